```python
import jax, jax.numpy as jnp
from jax import lax
import numpy as np

D_MODEL = 1024
BATCH = 1
SEQ = 16384
DEPTH = 1
DEC_BATCH = 128
DEC_SEQ = 4
PAST_LEN = 8192
PAGE_SIZE = 128

FOX_HEADS = 8
FOX_HEAD_DIM = D_MODEL // 16
FOX_WIDTH = FOX_HEADS * FOX_HEAD_DIM
QUERY_BLOCK = 128
MLSTM_HEADS = 4
MLSTM_HEAD_DIM = D_MODEL // 8
MLSTM_WIDTH = MLSTM_HEADS * MLSTM_HEAD_DIM
MLSTM_CHUNK = 64
MIX_WIDTH = FOX_WIDTH + MLSTM_WIDTH
IN_WIDTH = 3 * FOX_WIDTH + FOX_HEADS + 4 * MLSTM_WIDTH + 2 * MLSTM_HEADS
PEER_HEADS = 8
PEER_N_KEYS = 128
PEER_N_EXPERTS = PEER_N_KEYS * PEER_N_KEYS
PEER_TOPK = 16
PEER_KEY_DIM = 256
PEER_HALF = PEER_KEY_DIM // 2
PEER_BLOCK = 128
NORM_EPS = 1e-6

kernel_name = "hymba_fox_mlstm_peer_adaln_step"

F32 = jnp.float32


def rmsnorm(x, g):
    xf = x.astype(F32)
    y = xf * lax.rsqrt(jnp.mean(xf * xf, axis=-1, keepdims=True) + NORM_EPS)
    return (y * g.astype(F32)).astype(x.dtype)


def head_rmsnorm(x, g, n_heads):
    shp = x.shape
    xh = x.reshape(shp[:-1] + (n_heads, shp[-1] // n_heads)).astype(F32)
    y = xh * lax.rsqrt(jnp.mean(xh * xh, axis=-1, keepdims=True) + NORM_EPS)
    return (y.reshape(shp) * g.astype(F32)).astype(x.dtype)


def adaln_terms(c, w_ada, b_ada):
    ada = jax.nn.silu(c) @ w_ada + b_ada
    return [a[:, None, :] for a in jnp.split(ada, 6, axis=-1)]


def split_projection(z, b_fox_f, b_mlstm_i, b_mlstm_f):
    sizes = [FOX_WIDTH, FOX_WIDTH, FOX_WIDTH, FOX_HEADS,
             MLSTM_WIDTH, MLSTM_WIDTH, MLSTM_WIDTH, MLSTM_HEADS, MLSTM_HEADS, MLSTM_WIDTH]
    idx, acc = [], 0
    for s in sizes[:-1]:
        acc += s
        idx.append(acc)
    fq, fk, fv, ff, mq, mk, mv, mi, mf, mo = jnp.split(z, idx, axis=-1)
    B, T = z.shape[:2]
    fh = (B, T, FOX_HEADS, FOX_HEAD_DIM)
    mh = (B, T, MLSTM_HEADS, MLSTM_HEAD_DIM)
    fox_logf = jax.nn.log_sigmoid((ff + b_fox_f).astype(F32))
    m_i = (mi + b_mlstm_i).astype(F32)
    m_logf = jax.nn.log_sigmoid((mf + b_mlstm_f).astype(F32))
    return (fq.reshape(fh), fk.reshape(fh), fv.reshape(fh), fox_logf,
            mq.reshape(mh), mk.reshape(mh), mv.reshape(mh), m_i, m_logf, mo)


def fox_prompt(q, k, v, logf):
    B, T, H, Dh = q.shape
    scale = Dh ** -0.5
    F = jnp.cumsum(logf, axis=1)
    nb = T // QUERY_BLOCK
    qb = q.reshape(B, nb, QUERY_BLOCK, H, Dh).swapaxes(0, 1)
    Fb = F.reshape(B, nb, QUERY_BLOCK, H).swapaxes(0, 1)
    Fk = F.transpose(0, 2, 1)[:, :, None, :]
    key_pos = jnp.arange(T)

    def block(args):
        q_blk, F_blk, bi = args
        q_pos = bi * QUERY_BLOCK + jnp.arange(QUERY_BLOCK)
        s = jnp.einsum('bqhd,bkhd->bhqk', q_blk, k).astype(F32) * scale
        s = s + F_blk.transpose(0, 2, 1)[..., None] - Fk
        s = jnp.where(key_pos[None, :] <= q_pos[:, None], s, -jnp.inf)
        p = jax.nn.softmax(s, axis=-1).astype(v.dtype)
        return jnp.einsum('bhqk,bkhd->bqhd', p, v)

    o = lax.map(block, (qb, Fb, jnp.arange(nb)))
    return o.swapaxes(0, 1).reshape(B, T, H * Dh)


def fox_sample(q, k, v, logf, cache_k, cache_v, cache_logf, layer, page_table):
    NB, S, H, Dh = q.shape
    scale = Dh ** -0.5
    P = page_table.shape[1] * PAGE_SIZE
    kp = cache_k[layer, page_table].reshape(NB, P, H, Dh)
    vp = cache_v[layer, page_table].reshape(NB, P, H, Dh)
    lp = cache_logf[layer, page_table].reshape(NB, P, H).astype(F32)
    Fp = jnp.cumsum(lp, axis=1)
    Fn = Fp[:, -1:, :] + jnp.cumsum(logf, axis=1)
    Fq = Fn.transpose(0, 2, 1)[..., None]
    s_past = jnp.einsum('bqhd,bkhd->bhqk', q, kp).astype(F32) * scale + Fq - Fp.transpose(0, 2, 1)[:, :, None, :]
    s_new = jnp.einsum('bqhd,bkhd->bhqk', q, k).astype(F32) * scale + Fq - Fn.transpose(0, 2, 1)[:, :, None, :]
    s_new = jnp.where(jnp.tril(jnp.ones((S, S), bool)), s_new, -jnp.inf)
    p = jax.nn.softmax(jnp.concatenate([s_past, s_new], axis=-1), axis=-1).astype(v.dtype)
    o = jnp.einsum('bhqk,bkhd->bqhd', p[..., :P], vp) + jnp.einsum('bhqk,bkhd->bqhd', p[..., P:], v)
    return o.reshape(NB, S, H * Dh)


def mlstm_chunk(carry, xs):
    C, n, m = carry
    q, k, v, i_pre, logf = xs
    L = q.shape[2]
    b = jnp.cumsum(logf, axis=-1)
    causal = jnp.tril(jnp.ones((L, L), bool))
    logD = jnp.where(causal, b[..., :, None] - b[..., None, :] + i_pre[..., None, :], -jnp.inf)
    m_inter = b + m[..., None]
    m_t = jnp.maximum(m_inter, jnp.max(logD, axis=-1))
    Dmat = jnp.exp(logD - m_t[..., None])
    inter = jnp.exp(m_inter - m_t)
    Sm = jnp.einsum('bhtd,bhsd->bhts', q, k) * Dmat
    num = inter[..., None] * jnp.einsum('bhtd,bhde->bhte', q, C) + jnp.einsum('bhts,bhse->bhte', Sm, v)
    den = inter * jnp.einsum('bhtd,bhd->bht', q, n) + jnp.sum(Sm, axis=-1)
    h = num / jnp.maximum(jnp.abs(den), jnp.exp(-m_t))[..., None]
    m_new = m_t[..., -1]
    w_end = jnp.exp(b[..., -1:] - b + i_pre - m_new[..., None])
    carry_decay = jnp.exp(b[..., -1] + m - m_new)
    C_new = carry_decay[..., None, None] * C + jnp.einsum('bhs,bhsd,bhse->bhde', w_end, k, v)
    n_new = carry_decay[..., None] * n + jnp.einsum('bhs,bhsd->bhd', w_end, k)
    return (C_new, n_new, m_new), h


def mlstm_run(q, k, v, i_pre, logf, C0, n0, m0):
    B, T, H, Dk = q.shape
    L = MLSTM_CHUNK if T % MLSTM_CHUNK == 0 else T
    nc = T // L

    def chunks(a):
        a = a.astype(F32).reshape((B, nc, L) + a.shape[2:])
        return jnp.swapaxes(jnp.moveaxis(a, 1, 0), 2, 3)

    xs = (chunks(q), chunks(k * (Dk ** -0.5)), chunks(v), chunks(i_pre), chunks(logf))
    (C, n, m), h = lax.scan(mlstm_chunk, (C0.astype(F32), n0.astype(F32), m0.astype(F32)), xs)
    h = jnp.swapaxes(jnp.moveaxis(h, 0, 1), 2, 3).reshape(B, T, H * v.shape[-1])
    return h, C, n, m


def peer_ffn(h, w_peer_q, peer_keys, peer_u, peer_v):
    B, T, D = h.shape
    N = B * T
    nb = -(-N // PEER_BLOCK)
    xt = jnp.pad(h.reshape(N, D), ((0, nb * PEER_BLOCK - N), (0, 0))).reshape(nb, PEER_BLOCK, D)

    def block(xb):
        q = (xb @ w_peer_q).reshape(PEER_BLOCK, PEER_HEADS, 2, PEER_HALF)
        s = jnp.einsum('nhpd,hpkd->nhpk', q, peer_keys).astype(F32)
        s1, i1 = lax.top_k(s[:, :, 0], PEER_TOPK)
        s2, i2 = lax.top_k(s[:, :, 1], PEER_TOPK)
        cand_s = (s1[..., :, None] + s2[..., None, :]).reshape(PEER_BLOCK, PEER_HEADS, PEER_TOPK * PEER_TOPK)
        cand_i = (i1[..., :, None] * PEER_N_KEYS + i2[..., None, :]).reshape(PEER_BLOCK, PEER_HEADS, PEER_TOPK * PEER_TOPK)
        top_s, pos = lax.top_k(cand_s, PEER_TOPK)
        idx = jnp.take_along_axis(cand_i, pos, axis=-1)
        g = jax.nn.softmax(top_s, axis=-1)
        a = jax.nn.gelu(jnp.einsum('nd,nhkd->nhk', xb, peer_u[idx]).astype(F32), approximate=False)
        return jnp.einsum('nhk,nhkd->nd', (g * a).astype(xb.dtype), peer_v[idx])

    y = lax.map(block, xt).reshape(nb * PEER_BLOCK, D)[:N]
    return y.reshape(B, T, D)


def decoder_layer(x, c, w_ada, b_ada, g_norm_mix, g_norm_ffn, w_in, b_fox_f, b_mlstm_i, b_mlstm_f,
                  g_fox_out, g_mlstm_out, w_out, w_peer_q, peer_keys, peer_u, peer_v, fox_attend, C0, n0, m0):
    shift1, scale1, gate1, shift2, scale2, gate2 = adaln_terms(c, w_ada, b_ada)
    h = rmsnorm(x, g_norm_mix) * (1 + scale1) + shift1
    fq, fk, fv, flogf, mq, mk, mv, mi, mlogf, mo = split_projection(h @ w_in, b_fox_f, b_mlstm_i, b_mlstm_f)
    fox_o = fox_attend(fq, fk, fv, flogf)
    m_h, C, n, m = mlstm_run(mq, mk, mv, mi, mlogf, C0, n0, m0)
    a_out = head_rmsnorm(fox_o, g_fox_out, FOX_HEADS)
    b_out = jax.nn.sigmoid(mo) * head_rmsnorm(m_h.astype(x.dtype), g_mlstm_out, MLSTM_HEADS)
    x = x + gate1 * (jnp.concatenate([a_out, b_out], axis=-1) @ w_out)
    h2 = rmsnorm(x, g_norm_ffn) * (1 + scale2) + shift2
    x = x + gate2 * peer_ffn(h2, w_peer_q, peer_keys, peer_u, peer_v)
    return x, fk, fv, flogf, C, n, m


def setup_inputs(seed: int = 0) -> dict:
    key = jax.random.key(seed)
    ks = jax.random.split(key, 32)
    D = D_MODEL
    n_pages = PAST_LEN // PAGE_SIZE
    n_used = DEC_BATCH * n_pages
    n_pool = n_used + max(1, n_used // 4)
    perm = jax.random.permutation(ks[0], n_pool)
    page_table = perm[:n_used].reshape(DEC_BATCH, n_pages).astype(jnp.int32)

    def nrm(k, shape, s):
        return jax.random.normal(k, shape, F32) * s

    return {
        "x_prompt": nrm(ks[1], (BATCH, SEQ, D), 1.0),
        "x_sample": nrm(ks[2], (DEC_BATCH, DEC_SEQ, D), 1.0),
        "c_prompt": nrm(ks[3], (BATCH, D), 1.0),
        "c_sample": nrm(ks[4], (DEC_BATCH, D), 1.0),
        "cache_fox_k": nrm(ks[5], (DEPTH, n_pool, PAGE_SIZE, FOX_HEADS, FOX_HEAD_DIM), 1.0),
        "cache_fox_v": nrm(ks[6], (DEPTH, n_pool, PAGE_SIZE, FOX_HEADS, FOX_HEAD_DIM), 1.0),
        "cache_fox_logf": jax.nn.log_sigmoid(3.0 + nrm(ks[7], (DEPTH, n_pool, PAGE_SIZE, FOX_HEADS), 0.5)),
        "state_mlstm_C": nrm(ks[8], (DEPTH, DEC_BATCH, MLSTM_HEADS, MLSTM_HEAD_DIM, MLSTM_HEAD_DIM), 0.1),
        "state_mlstm_n": nrm(ks[9], (DEPTH, DEC_BATCH, MLSTM_HEADS, MLSTM_HEAD_DIM), 0.5),
        "state_mlstm_m": nrm(ks[10], (DEPTH, DEC_BATCH, MLSTM_HEADS), 1.0),
        "page_table": page_table,
        "w_ada": nrm(ks[11], (DEPTH, D, 6 * D), 0.5 * D ** -0.5),
        "b_ada": nrm(ks[12], (DEPTH, 6 * D), 0.02),
        "g_norm_mix": 1.0 + nrm(ks[13], (DEPTH, D), 0.02),
        "g_norm_ffn": 1.0 + nrm(ks[14], (DEPTH, D), 0.02),
        "w_in": nrm(ks[15], (DEPTH, D, IN_WIDTH), D ** -0.5),
        "b_fox_f": 3.0 + nrm(ks[16], (DEPTH, FOX_HEADS), 0.1),
        "b_mlstm_i": nrm(ks[17], (DEPTH, MLSTM_HEADS), 0.1),
        "b_mlstm_f": jnp.linspace(3.0, 6.0, MLSTM_HEADS, dtype=F32)[None, :] + nrm(ks[18], (DEPTH, MLSTM_HEADS), 0.1),
        "g_fox_out": 1.0 + nrm(ks[19], (DEPTH, FOX_WIDTH), 0.02),
        "g_mlstm_out": 1.0 + nrm(ks[20], (DEPTH, MLSTM_WIDTH), 0.02),
        "w_out": nrm(ks[21], (DEPTH, MIX_WIDTH, D), MIX_WIDTH ** -0.5),
        "w_peer_q": nrm(ks[22], (DEPTH, D, PEER_HEADS * PEER_KEY_DIM), D ** -0.5),
        "peer_keys": nrm(ks[23], (DEPTH, PEER_HEADS, 2, PEER_N_KEYS, PEER_HALF), PEER_HALF ** -0.5),
        "peer_u": nrm(ks[24], (DEPTH, PEER_N_EXPERTS, D), D ** -0.5),
        "peer_v": nrm(ks[25], (DEPTH, PEER_N_EXPERTS, D), 0.1),
        "g_final": 1.0 + nrm(ks[26], (D,), 0.02),
    }


def reference(x_prompt, x_sample, c_prompt, c_sample, cache_fox_k, cache_fox_v, cache_fox_logf,
              state_mlstm_C, state_mlstm_n, state_mlstm_m, page_table,
              w_ada, b_ada, g_norm_mix, g_norm_ffn, w_in, b_fox_f, b_mlstm_i, b_mlstm_f,
              g_fox_out, g_mlstm_out, w_out, w_peer_q, peer_keys, peer_u, peer_v, g_final):
    xp, xs = x_prompt, x_sample
    kp_l, vp_l, lp_l, Cp_l, np_l, mp_l = [], [], [], [], [], []
    ks_l, vs_l, ls_l, Cs_l, ns_l, ms_l = [], [], [], [], [], []
    for l in range(DEPTH):
        w_layer = (w_ada[l], b_ada[l], g_norm_mix[l], g_norm_ffn[l], w_in[l], b_fox_f[l], b_mlstm_i[l],
                   b_mlstm_f[l], g_fox_out[l], g_mlstm_out[l], w_out[l], w_peer_q[l], peer_keys[l],
                   peer_u[l], peer_v[l])
        C0 = jnp.zeros((xp.shape[0], MLSTM_HEADS, MLSTM_HEAD_DIM, MLSTM_HEAD_DIM), F32)
        n0 = jnp.zeros((xp.shape[0], MLSTM_HEADS, MLSTM_HEAD_DIM), F32)
        m0 = jnp.zeros((xp.shape[0], MLSTM_HEADS), F32)
        xp, fk, fv, fl, C, n, m = decoder_layer(xp, c_prompt, *w_layer, fox_prompt, C0, n0, m0)
        kp_l.append(fk); vp_l.append(fv); lp_l.append(fl); Cp_l.append(C); np_l.append(n); mp_l.append(m)

        def fox_attend_sample(q, k, v, lf, layer=l):
            return fox_sample(q, k, v, lf, cache_fox_k, cache_fox_v, cache_fox_logf, layer, page_table)

        xs, fk, fv, fl, C, n, m = decoder_layer(xs, c_sample, *w_layer, fox_attend_sample,
                                                state_mlstm_C[l], state_mlstm_n[l], state_mlstm_m[l])
        ks_l.append(fk); vs_l.append(fv); ls_l.append(fl); Cs_l.append(C); ns_l.append(n); ms_l.append(m)

    y_prompt = rmsnorm(xp, g_final)
    y_sample = rmsnorm(xs, g_final)
    return (y_prompt, y_sample,
            jnp.stack(kp_l), jnp.stack(vp_l), jnp.stack(lp_l), jnp.stack(Cp_l), jnp.stack(np_l), jnp.stack(mp_l),
            jnp.stack(ks_l), jnp.stack(vs_l), jnp.stack(ls_l), jnp.stack(Cs_l), jnp.stack(ns_l), jnp.stack(ms_l))
```

```python
import functools
import math

import jax
import jax.numpy as jnp
from jax import lax
from jax.experimental import pallas as pl
from jax.experimental.pallas import tpu as pltpu

F32 = jnp.float32
BF16 = jnp.bfloat16
HIGHEST = lax.Precision.HIGHEST

D_MODEL = 1024
FOX_HEADS = 8
FOX_HEAD_DIM = 64
FOX_WIDTH = FOX_HEADS * FOX_HEAD_DIM
MLSTM_HEADS = 4
MLSTM_HEAD_DIM = 128
MLSTM_WIDTH = MLSTM_HEADS * MLSTM_HEAD_DIM
PAGE_SIZE = 128
PEER_HEADS = 8
PEER_N_KEYS = 128
PEER_TOPK = 16
PEER_HALF = 128
NORM_EPS = 1e-6
GATE_LANES = 128
NEG_BIG = -1e30
LOG2E = 1.4426950408889634
VMEM_LIMIT = 56 * 1024 * 1024


def _cparams(sem):
    return pltpu.CompilerParams(dimension_semantics=sem, vmem_limit_bytes=VMEM_LIMIT)


def _log_sigmoid(x):
    return jnp.minimum(x, 0.0) - jnp.log1p(jnp.exp(-jnp.abs(x)))


def _adaln_kernel(c_ref, w_ref, b_ref, o_ref):
    c = c_ref[...]
    s = c / (1.0 + jnp.exp(-c))
    o_ref[...] = jnp.dot(s, w_ref[...], precision=HIGHEST, preferred_element_type=F32) + b_ref[...]


def adaln_terms(c, w_ada, b_ada):
    rows, d = c.shape
    n = w_ada.shape[1]
    tn = 768
    return pl.pallas_call(
        _adaln_kernel,
        out_shape=jax.ShapeDtypeStruct((rows, n), F32),
        grid=(n // tn,),
        in_specs=[pl.BlockSpec((rows, d), lambda j: (0, 0)),
                  pl.BlockSpec((d, tn), lambda j: (0, j)),
                  pl.BlockSpec((1, tn), lambda j: (0, j))],
        out_specs=pl.BlockSpec((rows, tn), lambda j: (0, j)),
        compiler_params=_cparams(("arbitrary",)),
        name="adaln",
    )(c, w_ada, b_ada.reshape(1, n))


def _inproj_kernel(x_ref, sc_ref, sh_ref, g_ref, wm_ref, wg_ref, bg_ref,
                   fq_ref, fk_ref, fv_ref, mq_ref, mk_ref, mv_ref, mo_ref, gt_ref, gtt_ref):
    x = x_ref[...]
    y = x * lax.rsqrt(jnp.mean(x * x, axis=-1, keepdims=True) + NORM_EPS) * g_ref[...]
    h = y * (1.0 + sc_ref[...]) + sh_ref[...]
    hb = h.astype(BF16)
    for i, o_ref in enumerate((fq_ref, fk_ref, fv_ref, mq_ref, mk_ref, mv_ref, mo_ref)):
        o_ref[...] = jnp.dot(hb, wm_ref[:, i * 512:(i + 1) * 512], preferred_element_type=F32)
    zg = jnp.dot(h, wg_ref[...], precision=HIGHEST, preferred_element_type=F32) + bg_ref[...]
    col = lax.broadcasted_iota(jnp.int32, zg.shape, 1)
    is_i = (col >= FOX_HEADS) & (col < FOX_HEADS + MLSTM_HEADS)
    gt = jnp.where(is_i, zg, _log_sigmoid(zg))
    gt_ref[...] = gt
    gtt_ref[...] = gt.T


def in_projection(x, scale, shift, g, wm, wg, bg, tm):
    n, d = x.shape
    mod_rows = scale.shape[0]
    if mod_rows == 1:
        mod_spec = pl.BlockSpec((1, d), lambda i: (0, 0))
    else:
        mod_spec = pl.BlockSpec((tm, d), lambda i: (i, 0))
    o512 = jax.ShapeDtypeStruct((n, 512), F32)
    s512 = pl.BlockSpec((tm, 512), lambda i: (i, 0))
    return pl.pallas_call(
        _inproj_kernel,
        out_shape=[o512] * 7 + [jax.ShapeDtypeStruct((n, GATE_LANES), F32),
                                jax.ShapeDtypeStruct((GATE_LANES, n), F32)],
        grid=(n // tm,),
        in_specs=[pl.BlockSpec((tm, d), lambda i: (i, 0)), mod_spec, mod_spec,
                  pl.BlockSpec((1, d), lambda i: (0, 0)),
                  pl.BlockSpec(wm.shape, lambda i: (0, 0)),
                  pl.BlockSpec(wg.shape, lambda i: (0, 0)),
                  pl.BlockSpec((1, GATE_LANES), lambda i: (0, 0))],
        out_specs=[s512] * 7 + [pl.BlockSpec((tm, GATE_LANES), lambda i: (i, 0)),
                                pl.BlockSpec((GATE_LANES, tm), lambda i: (0, i))],
        compiler_params=_cparams(("arbitrary",)),
        name="inproj",
    )(x, scale, shift, g, wm, wg, bg)


def _lane_cumsum(x):
    n = x.shape[-1]
    lane = lax.broadcasted_iota(jnp.int32, x.shape, x.ndim - 1)
    sh = 1
    while sh < n:
        x = x + jnp.where(lane >= sh, pltpu.roll(x, sh, x.ndim - 1), 0.0)
        sh *= 2
    return x


def _cumsum_kernel(x_ref, o_ref):
    o_ref[...] = _lane_cumsum(x_ref[...])


def lane_cumsum(x):
    return pl.pallas_call(
        _cumsum_kernel,
        out_shape=jax.ShapeDtypeStruct(x.shape, F32),
        name="cumsum",
    )(x)


def _flash_kernel(qi_ref, ki_ref, q_ref, k_ref, v_ref, fk_ref, fq_ref, o_ref,
                  m_ref, l_ref, acc_ref, *, tq, tk):
    s_idx = pl.program_id(1)
    qi = qi_ref[s_idx]
    ki = ki_ref[s_idx]

    @pl.when(ki == 0)
    def _():
        m_ref[...] = jnp.full(m_ref.shape, -jnp.inf, F32)
        l_ref[...] = jnp.zeros(l_ref.shape, F32)
        acc_ref[...] = jnp.zeros(acc_ref.shape, F32)

    def step(masked):
        q = q_ref[0]
        k = k_ref[0]
        s = lax.dot_general(q, k, (((1,), (1,)), ((), ())), preferred_element_type=F32)
        bias = fq_ref[0, :, 0:1] - fk_ref[0]
        s = s + bias
        if masked:
            row = lax.broadcasted_iota(jnp.int32, s.shape, 0)
            col = lax.broadcasted_iota(jnp.int32, s.shape, 1)
            s = jnp.where(col <= row, s, -jnp.inf)
        m_prev = m_ref[...]
        m_new = jnp.maximum(m_prev, jnp.max(s, axis=-1, keepdims=True))
        alpha = jnp.exp(m_prev - m_new)
        p = jnp.exp(s - m_new)
        l_ref[...] = alpha * l_ref[...] + jnp.sum(p, axis=-1, keepdims=True)
        acc_ref[...] = alpha * acc_ref[...] + jnp.dot(p.astype(BF16), v_ref[0],
                                                      preferred_element_type=F32)
        m_ref[...] = m_new

    @pl.when(ki < qi)
    def _():
        step(False)

    @pl.when(ki == qi)
    def _():
        step(True)
        o_ref[0] = acc_ref[...] / l_ref[...]


def fox_prompt_attention(q, k, v, fkt, t_blk):
    h, t, dh = q.shape
    nb = t // t_blk
    pairs = [(i, j) for i in range(nb) for j in range(i + 1)]
    qi_arr = jnp.array([p[0] for p in pairs], jnp.int32)
    ki_arr = jnp.array([p[1] for p in pairs], jnp.int32)
    grid_spec = pltpu.PrefetchScalarGridSpec(
        num_scalar_prefetch=2,
        grid=(h, len(pairs)),
        in_specs=[pl.BlockSpec((1, t_blk, dh), lambda hh, s, qi, ki: (hh, qi[s], 0)),
                  pl.BlockSpec((1, t_blk, dh), lambda hh, s, qi, ki: (hh, ki[s], 0)),
                  pl.BlockSpec((1, t_blk, dh), lambda hh, s, qi, ki: (hh, ki[s], 0)),
                  pl.BlockSpec((1, 1, t_blk), lambda hh, s, qi, ki: (hh, 0, ki[s])),
                  pl.BlockSpec((1, 1, t_blk), lambda hh, s, qi, ki: (hh, 0, qi[s]))],
        out_specs=pl.BlockSpec((1, t_blk, dh), lambda hh, s, qi, ki: (hh, qi[s], 0)),
        scratch_shapes=[pltpu.VMEM((t_blk, 1), F32), pltpu.VMEM((t_blk, 1), F32),
                        pltpu.VMEM((t_blk, dh), F32)],
    )
    return pl.pallas_call(
        functools.partial(_flash_kernel, tq=t_blk, tk=t_blk),
        out_shape=jax.ShapeDtypeStruct((h, t, dh), F32),
        grid_spec=grid_spec,
        compiler_params=_cparams(("arbitrary", "arbitrary")),
        name="fox_prompt",
    )(qi_arr, ki_arr, q, k, v, fkt, fkt)


def _mlstm_kernel(q_ref, k_ref, v_ref, g_ref, gt_ref, c0_ref, n0_ref, m0_ref,
                  h_ref, c_out_ref, n_out_ref, m_out_ref, c_s, n_s, m_s, *, rows, seg_len):
    nseg = rows // seg_len
    dk = MLSTM_HEAD_DIM
    c_idx = pl.program_id(1)

    @pl.when(c_idx == 0)
    def _():
        c_s[...] = c0_ref[...]
        n_s[...] = n0_ref[...]
        m_s[...] = m0_ref[...]

    g = g_ref[0]
    gt = gt_ref[0]
    row = lax.broadcasted_iota(jnp.int32, (rows, rows), 0)
    col = lax.broadcasted_iota(jnp.int32, (rows, rows), 1)
    if nseg == 1:
        causal = col <= row
        anti = row <= col
    else:
        same = (row // seg_len) == (col // seg_len)
        causal = (col <= row) & same
        anti = (row <= col) & same
    rid = lax.broadcasted_iota(jnp.int32, (rows, 1), 0) // seg_len

    def per_row(vals):
        out = jnp.broadcast_to(vals[0], (rows, 1))
        for j in range(1, nseg):
            out = jnp.where(rid == j, vals[j], out)
        return out

    for h in range(MLSTM_HEADS):
        lane = slice(h * dk, (h + 1) * dk)
        q = q_ref[0, :, lane]
        k = k_ref[0, :, lane] * (dk ** -0.5)
        v = v_ref[0, :, lane]
        i_row = gt[8 + h:9 + h, :]
        lf_row = gt[12 + h:13 + h, :]
        i_col = g[:, 8 + h:9 + h]
        lf_col = g[:, 12 + h:13 + h]
        b_col = jnp.sum(jnp.where(causal, lf_row, 0.0), axis=1, keepdims=True)
        b_row = jnp.sum(jnp.where(anti, lf_col, 0.0), axis=0, keepdims=True)
        log_d = jnp.where(causal, b_col - b_row + i_row, -jnp.inf)
        m_prev = [m_s[j, h] for j in range(nseg)]
        m_inter = b_col + per_row(m_prev)
        m_t = jnp.maximum(m_inter, jnp.max(log_d, axis=1, keepdims=True))
        d_mat = jnp.exp(log_d - m_t)
        inter = jnp.exp(m_inter - m_t)
        qb = q.astype(BF16)
        s_mat = lax.dot_general(qb, k.astype(BF16), (((1,), (1,)), ((), ())),
                                preferred_element_type=F32) * d_mat
        num = jnp.dot(s_mat.astype(BF16), v.astype(BF16), preferred_element_type=F32)
        qc = jnp.concatenate(
            [jnp.dot(qb[j * seg_len:(j + 1) * seg_len], c_s[j, h].astype(BF16),
                     preferred_element_type=F32) for j in range(nseg)], axis=0)
        n_rows = jnp.concatenate(
            [jnp.broadcast_to(n_s[j, h], (seg_len, dk)) for j in range(nseg)], axis=0)
        qn = jnp.sum(q * n_rows, axis=1, keepdims=True)
        num = inter * qc + num
        den = inter * qn + jnp.sum(s_mat, axis=1, keepdims=True)
        h_ref[0, :, lane] = num / jnp.maximum(jnp.abs(den), jnp.exp(-m_t))

        last = [(j + 1) * seg_len - 1 for j in range(nseg)]
        b_last = [b_col[r:r + 1, :] for r in last]
        m_new = [m_t[r:r + 1, :] for r in last]
        w_end = jnp.exp(per_row(b_last) - b_col + i_col - per_row(m_new))
        kw = k * w_end
        kwt = kw.T.astype(BF16)
        vb = v.astype(BF16)
        if nseg > 1:
            vb = jnp.concatenate([jnp.where(rid == j, vb, jnp.zeros_like(vb))
                                  for j in range(nseg)], axis=1)
        upd = jnp.dot(kwt, vb, preferred_element_type=F32)
        for j in range(nseg):
            decay = jnp.exp(b_last[j] + m_prev[j] - m_new[j])
            kw_j = kw if nseg == 1 else jnp.where(rid == j, kw, 0.0)
            c_s[j, h] = decay * c_s[j, h] + upd[:, j * dk:(j + 1) * dk]
            n_s[j, h] = decay * n_s[j, h] + jnp.sum(kw_j, axis=0, keepdims=True)
            m_s[j, h] = m_new[j]

    @pl.when(c_idx == pl.num_programs(1) - 1)
    def _():
        c_out_ref[...] = c_s[...]
        n_out_ref[...] = n_s[...]
        m_out_ref[...] = m_s[...]


def mlstm(q, k, v, gates, gates_t, c0, n0, m0, rows, seg_len):
    G, tg, w = q.shape
    nseg = rows // seg_len
    nc = tg // rows
    assert nseg == 1 or nc == 1
    qspec = pl.BlockSpec((1, rows, w), lambda gi, c: (gi, c, 0))
    cspec = pl.BlockSpec((nseg, MLSTM_HEADS, 128, 128), lambda gi, c: (gi, 0, 0, 0))
    nspec = pl.BlockSpec((nseg, MLSTM_HEADS, 1, 128), lambda gi, c: (gi, 0, 0, 0))
    mspec = pl.BlockSpec((nseg, MLSTM_HEADS, 1, 1), lambda gi, c: (gi, 0, 0, 0))
    return pl.pallas_call(
        functools.partial(_mlstm_kernel, rows=rows, seg_len=seg_len),
        out_shape=[jax.ShapeDtypeStruct((G, tg, w), F32),
                   jax.ShapeDtypeStruct(c0.shape, F32),
                   jax.ShapeDtypeStruct(n0.shape, F32),
                   jax.ShapeDtypeStruct(m0.shape, F32)],
        grid=(G, nc),
        in_specs=[qspec, qspec, qspec,
                  pl.BlockSpec((1, rows, GATE_LANES), lambda gi, c: (gi, c, 0)),
                  pl.BlockSpec((1, 16, rows), lambda gi, c: (gi, 0, c)),
                  cspec, nspec, mspec],
        out_specs=[qspec, cspec, nspec, mspec],
        scratch_shapes=[pltpu.VMEM((nseg, MLSTM_HEADS, 128, 128), F32),
                        pltpu.VMEM((nseg, MLSTM_HEADS, 1, 128), F32),
                        pltpu.VMEM((nseg, MLSTM_HEADS, 1, 1), F32)],
        compiler_params=_cparams(("arbitrary", "arbitrary")),
        name="mlstm",
    )(q, k, v, gates, gates_t, c0, n0, m0)


def _fox_sample_kernel(pt_ref, qbd_ref, knew_ref, vnew_ref, lfnew_ref, *rest, pps):
    k_refs = rest[0:pps]
    v_refs = rest[pps:2 * pps]
    lf_refs = rest[2 * pps:3 * pps]
    o_ref = rest[3 * pps]
    m_ref, l_ref, acc_ref, f_ref = rest[3 * pps + 1:]
    step = pl.program_id(1)
    nq = qbd_ref.shape[1]
    reps = nq // FOX_HEADS

    @pl.when(step == 0)
    def _():
        m_ref[...] = jnp.full(m_ref.shape, -jnp.inf, F32)
        l_ref[...] = jnp.zeros(l_ref.shape, F32)
        acc_ref[...] = jnp.zeros(acc_ref.shape, F32)
        f_ref[...] = jnp.zeros(f_ref.shape, F32)

    qbd = qbd_ref[0]

    def scores(k_page, f_page):
        s = lax.dot_general(qbd, k_page.astype(BF16), (((1,), (1,)), ((), ())),
                            preferred_element_type=F32)
        return s - jnp.concatenate([f_page] * reps, axis=0)

    def update(logits, v_pages):
        m_prev = m_ref[...]
        m_new = jnp.maximum(m_prev, jnp.max(logits, axis=-1, keepdims=True))
        alpha = jnp.exp(m_prev - m_new)
        p = jnp.exp(logits - m_new)
        l_ref[...] = alpha * l_ref[...] + jnp.sum(p, axis=-1, keepdims=True)
        pb = p.astype(BF16)
        pv = jnp.dot(pb[:, 0:PAGE_SIZE], v_pages[0].astype(BF16), preferred_element_type=F32)
        for i in range(1, len(v_pages)):
            pv += jnp.dot(pb[:, i * PAGE_SIZE:(i + 1) * PAGE_SIZE], v_pages[i].astype(BF16),
                          preferred_element_type=F32)
        acc_ref[...] = alpha * acc_ref[...] + pv
        m_ref[...] = m_new

    carry = f_ref[...]
    logits = []
    for i in range(pps):
        f_page = _lane_cumsum(lf_refs[i][0]) + carry
        carry = f_page[:, PAGE_SIZE - 1:PAGE_SIZE]
        logits.append(scores(k_refs[i][0], f_page))
    f_ref[...] = carry
    update(jnp.concatenate(logits, axis=1), [r[0] for r in v_refs])

    @pl.when(step == pl.num_programs(1) - 1)
    def _():
        pad = jnp.zeros((PAGE_SIZE - knew_ref.shape[1], knew_ref.shape[2]), F32)
        k_new = jnp.concatenate([knew_ref[0], pad], axis=0)
        v_new = jnp.concatenate([vnew_ref[0], pad], axis=0)
        f_new = _lane_cumsum(lfnew_ref[0]) + carry
        lg = scores(k_new, f_new)
        row = lax.broadcasted_iota(jnp.int32, lg.shape, 0)
        col = lax.broadcasted_iota(jnp.int32, lg.shape, 1)
        lg = jnp.where(col <= row // FOX_HEADS, lg, -jnp.inf)
        update(lg, [v_new])
        o = acc_ref[...] / l_ref[...]
        row = lax.broadcasted_iota(jnp.int32, o.shape, 0)
        col = lax.broadcasted_iota(jnp.int32, o.shape, 1)
        o = jnp.where(col // FOX_HEAD_DIM == row % FOX_HEADS, o, 0.0)
        o_ref[0] = jnp.concatenate(
            [jnp.sum(o[j * FOX_HEADS:(j + 1) * FOX_HEADS], axis=0, keepdims=True)
             for j in range(reps)], axis=0)


def fox_sample_attention(page_table, qbd, k_new, v_new, lf_new_t, k_pool, v_pool, lf_pool_t, pps):
    b, nq, w = qbd.shape
    n_pages = page_table.shape[1]
    pt = page_table.reshape(-1)
    s_new = nq // FOX_HEADS

    def page_map(i):
        return lambda bi, s, pt_ref: (pt_ref[bi * n_pages + s * pps + i], 0, 0)

    def bmap(bi, s, pt_ref):
        return (bi, 0, 0)

    in_specs = [pl.BlockSpec((1, nq, w), bmap),
                pl.BlockSpec((1,) + k_new.shape[1:], bmap),
                pl.BlockSpec((1,) + v_new.shape[1:], bmap),
                pl.BlockSpec((1,) + lf_new_t.shape[1:], bmap)]
    in_specs += [pl.BlockSpec((1, PAGE_SIZE, w), page_map(i)) for i in range(pps)]
    in_specs += [pl.BlockSpec((1, PAGE_SIZE, w), page_map(i)) for i in range(pps)]
    in_specs += [pl.BlockSpec((1, FOX_HEADS, PAGE_SIZE), page_map(i)) for i in range(pps)]
    grid_spec = pltpu.PrefetchScalarGridSpec(
        num_scalar_prefetch=1,
        grid=(b, n_pages // pps),
        in_specs=in_specs,
        out_specs=pl.BlockSpec((1, s_new, w), bmap),
        scratch_shapes=[pltpu.VMEM((nq, 1), F32), pltpu.VMEM((nq, 1), F32),
                        pltpu.VMEM((nq, w), F32), pltpu.VMEM((FOX_HEADS, 1), F32)],
    )
    return pl.pallas_call(
        functools.partial(_fox_sample_kernel, pps=pps),
        out_shape=jax.ShapeDtypeStruct((b, s_new, w), F32),
        grid_spec=grid_spec,
        compiler_params=_cparams(("arbitrary", "arbitrary")),
        name="fox_sample",
    )(pt, qbd, k_new, v_new, lf_new_t, *([k_pool] * pps), *([v_pool] * pps), *([lf_pool_t] * pps))


def _group_mean_sq(x, bd):
    sq = x * x
    hi = sq.astype(BF16)
    lo = (sq - hi.astype(F32)).astype(BF16)
    return (jnp.dot(hi, bd, preferred_element_type=F32) + jnp.dot(lo, bd, preferred_element_type=F32))


def _combine_kernel(fo_ref, mh_ref, mo_ref, x_ref, g1_ref, sc2_ref, sh2_ref, gfo_ref, gml_ref,
                    gffn_ref, wout_ref, wpq_ref, keys_ref, bd64_ref, bd128_ref,
                    x1_ref, h2_ref, st_ref):
    fo = fo_ref[...]
    a_out = fo * lax.rsqrt(_group_mean_sq(fo, bd64_ref[...]) + NORM_EPS) * gfo_ref[...]
    mh = mh_ref[...]
    mo = mo_ref[...]
    b_out = (mh * lax.rsqrt(_group_mean_sq(mh, bd128_ref[...]) + NORM_EPS) * gml_ref[...]
             / (1.0 + jnp.exp(-mo)))
    cat = jnp.concatenate([a_out, b_out], axis=1).astype(BF16)
    x1 = x_ref[...] + g1_ref[...] * jnp.dot(cat, wout_ref[...], preferred_element_type=F32)
    x1_ref[...] = x1
    y = x1 * lax.rsqrt(jnp.mean(x1 * x1, axis=-1, keepdims=True) + NORM_EPS) * gffn_ref[...]
    h2 = (y * (1.0 + sc2_ref[...]) + sh2_ref[...]).astype(BF16)
    h2_ref[...] = h2
    pq = jnp.dot(h2, wpq_ref[...], preferred_element_type=F32).astype(BF16)
    for hp in range(2 * PEER_HEADS):
        st_ref[hp] = lax.dot_general(keys_ref[hp], pq[:, hp * PEER_HALF:(hp + 1) * PEER_HALF],
                                     (((1,), (1,)), ((), ())), preferred_element_type=F32)


def combine(fox_o, m_h, mo, x, gate1, scale2, shift2, g_fox, g_ml, g_ffn, w_out, w_pq, keys,
            bd64, bd128, tm):
    n, d = x.shape
    mod_rows = gate1.shape[0]
    if mod_rows == 1:
        mod_spec = pl.BlockSpec((1, d), lambda i: (0, 0))
    else:
        mod_spec = pl.BlockSpec((tm, d), lambda i: (i, 0))

    def full(a):
        return pl.BlockSpec(a.shape, lambda i: (0,) * a.ndim)

    s512 = pl.BlockSpec((tm, 512), lambda i: (i, 0))
    sd = pl.BlockSpec((tm, d), lambda i: (i, 0))
    return pl.pallas_call(
        _combine_kernel,
        out_shape=[jax.ShapeDtypeStruct((n, d), F32), jax.ShapeDtypeStruct((n, d), BF16),
                   jax.ShapeDtypeStruct((2 * PEER_HEADS, PEER_N_KEYS, n), F32)],
        grid=(n // tm,),
        in_specs=[s512, s512, s512, sd, mod_spec, mod_spec, mod_spec, full(g_fox), full(g_ml),
                  full(g_ffn), full(w_out), full(w_pq), full(keys), full(bd64), full(bd128)],
        out_specs=[sd, sd, pl.BlockSpec((2 * PEER_HEADS, PEER_N_KEYS, tm), lambda i: (0, 0, i))],
        compiler_params=_cparams(("arbitrary",)),
        name="combine",
    )(fox_o, m_h, mo, x, gate1, scale2, shift2, g_fox, g_ml, g_ffn, w_out, w_pq, keys, bd64, bd128)


def _sort_desc(a):
    a = list(a)
    n = len(a)
    k = 2
    while k <= n:
        j = k // 2
        while j >= 1:
            for i in range(n):
                p = i ^ j
                if p > i:
                    hi, lo = jnp.maximum(a[i], a[p]), jnp.minimum(a[i], a[p])
                    a[i], a[p] = (hi, lo) if (i & k) == 0 else (lo, hi)
            j //= 2
        k *= 2
    return a


def _merge_desc(a):
    a = list(a)
    j = len(a) // 2
    while j >= 1:
        for i in range(len(a)):
            p = i ^ j
            if p > i:
                a[i], a[p] = jnp.maximum(a[i], a[p]), jnp.minimum(a[i], a[p])
        j //= 2
    return a


_PEER_CAND = [(i, j) for i in range(PEER_TOPK) for j in range(PEER_TOPK)
              if (i + 1) * (j + 1) <= PEER_TOPK]


def _top16_of_128(slabs):
    t = _sort_desc(slabs)
    sub = lax.broadcasted_iota(jnp.int32, t[0].shape, 0)
    for bit in (1, 2, 4):
        if bit == 4:
            partner = [pltpu.roll(v, 4, 0) for v in t]
        else:
            low = (sub & bit) == 0
            partner = [jnp.where(low, pltpu.roll(v, 8 - bit, 0), pltpu.roll(v, bit, 0)) for v in t]
        t = _merge_desc([jnp.maximum(t[i], partner[PEER_TOPK - 1 - i]) for i in range(PEER_TOPK)])
    return t


def _kth_candidate_sums(t1, t2):
    c = [t1[i] + t2[j] for (i, j) in _PEER_CAND]
    c += [jnp.full(c[0].shape, NEG_BIG, F32)] * (64 - len(c))
    return _sort_desc(c)


def _topk_kernel(st_ref, u1_ref, u2_ref, th_ref, t1_s, t2_s):
    nslab = PEER_N_KEYS // 8

    def pack(t, t_s, h):
        for i in range(PEER_TOPK):
            t_s[i, h:h + 1, :] = t[i][0:1, :]

    for h in range(PEER_HEADS):
        u1_ref[h] = st_ref[2 * h] * LOG2E
        u2_ref[h] = st_ref[2 * h + 1] * LOG2E
        pack(_top16_of_128([u1_ref[h, 8 * i:8 * i + 8, :] for i in range(nslab)]), t1_s, h)
        pack(_top16_of_128([u2_ref[h, 8 * i:8 * i + 8, :] for i in range(nslab)]), t2_s, h)
    t1 = [t1_s[i] for i in range(PEER_TOPK)]
    c = _kth_candidate_sums(t1, [t2_s[i] for i in range(PEER_TOPK)])
    z = jnp.exp2(c[0] - c[0])
    for r in range(1, PEER_TOPK):
        z = z + jnp.exp2(c[r] - c[0])
    k2 = c[0] + jnp.log2(z)
    for h in range(PEER_HEADS):
        u2_ref[h] = u2_ref[h] - k2[h:h + 1, :]
        pack(_top16_of_128([u2_ref[h, 8 * i:8 * i + 8, :] for i in range(nslab)]), t2_s, h)
    c = _kth_candidate_sums(t1, [t2_s[i] for i in range(PEER_TOPK)])
    th_ref[...] = c[PEER_TOPK - 1]


def peer_topk(st):
    tm = 128
    hp, nk, n = st.shape
    uspec = pl.BlockSpec((PEER_HEADS, nk, tm), lambda i: (0, 0, i))
    ushape = jax.ShapeDtypeStruct((PEER_HEADS, nk, n), F32)
    return pl.pallas_call(
        _topk_kernel,
        out_shape=[ushape, ushape, jax.ShapeDtypeStruct((PEER_HEADS, n), F32)],
        grid=(n // tm,),
        in_specs=[pl.BlockSpec((hp, nk, tm), lambda i: (0, 0, i))],
        out_specs=[uspec, uspec, pl.BlockSpec((PEER_HEADS, tm), lambda i: (0, i))],
        scratch_shapes=[pltpu.VMEM((PEER_TOPK, 8, 128), F32), pltpu.VMEM((PEER_TOPK, 8, 128), F32)],
        compiler_params=_cparams(("arbitrary",)),
        name="peer_topk",
    )(st)


def _gelu(x):
    return 0.5 * x * (1.0 + lax.erf(x * (2.0 ** -0.5)))


def _peer_kernel(u1_ref, u2_ref, th_ref, h2_ref, u_ref, vt_ref, x1_ref, g2_ref, gf_ref, o_ref,
                 acc_ref, at_ref, p_ref, *, tm, te):
    e = pl.program_id(1)

    @pl.when(e == 0)
    def _():
        acc_ref[...] = jnp.zeros(acc_ref.shape, F32)

    at_ref[...] = lax.dot_general(u_ref[...], h2_ref[...], (((1,), (1,)), ((), ())),
                                  preferred_element_type=F32)
    a_per_blk = te // PEER_N_KEYS

    for al in range(a_per_blk):
        rows = slice(al * PEER_N_KEYS, (al + 1) * PEER_N_KEYS)
        for tg in range(tm // 128):
            lanes = slice(tg * 128, (tg + 1) * 128)
            w = jnp.zeros((PEER_N_KEYS, 128), F32)
            for h in range(PEER_HEADS):
                c = u1_ref[h, al:al + 1, lanes] + u2_ref[h, :, lanes]
                w = w + jnp.where(c >= th_ref[h:h + 1, lanes], jnp.exp2(c), 0.0)
            p_ref[rows, lanes] = (w * _gelu(at_ref[rows, lanes])).astype(BF16)
    acc_ref[...] += jnp.dot(vt_ref[...], p_ref[...], preferred_element_type=F32)

    @pl.when(e == pl.num_programs(1) - 1)
    def _():
        xo = x1_ref[...] + g2_ref[...] * acc_ref[...].T
        o_ref[...] = xo * lax.rsqrt(jnp.mean(xo * xo, axis=-1, keepdims=True) + NORM_EPS) * gf_ref[...]


def peer_experts(u1, u2, th, h2, u_bf, vt_bf, x1, gate2, g_final, tm, te):
    n, d = x1.shape
    n_exp = u_bf.shape[0]
    mod_rows = gate2.shape[0]
    if mod_rows == 1:
        mod_spec = pl.BlockSpec((1, d), lambda i, e: (0, 0))
    else:
        mod_spec = pl.BlockSpec((tm, d), lambda i, e: (i, 0))
    uspec = pl.BlockSpec((PEER_HEADS, PEER_N_KEYS, tm), lambda i, e: (0, 0, i))
    u1spec = pl.BlockSpec((PEER_HEADS, te // PEER_N_KEYS, tm), lambda i, e: (0, e, i))
    return pl.pallas_call(
        functools.partial(_peer_kernel, tm=tm, te=te),
        out_shape=jax.ShapeDtypeStruct((n, d), F32),
        grid=(n // tm, n_exp // te),
        in_specs=[u1spec, uspec, pl.BlockSpec((PEER_HEADS, tm), lambda i, e: (0, i)),
                  pl.BlockSpec((tm, d), lambda i, e: (i, 0)),
                  pl.BlockSpec((te, d), lambda i, e: (e, 0)),
                  pl.BlockSpec((d, te), lambda i, e: (0, e)),
                  pl.BlockSpec((tm, d), lambda i, e: (i, 0)), mod_spec,
                  pl.BlockSpec((1, d), lambda i, e: (0, 0))],
        out_specs=pl.BlockSpec((tm, d), lambda i, e: (i, 0)),
        scratch_shapes=[pltpu.VMEM((d, tm), F32), pltpu.VMEM((te, tm), F32),
                        pltpu.VMEM((te, tm), BF16)],
        compiler_params=_cparams(("arbitrary", "arbitrary")),
        name="peer_experts",
    )(u1, u2, th, h2, u_bf, vt_bf, x1, gate2, g_final)


def _prep_weights(w_in, b_fox_f, b_mlstm_i, b_mlstm_f):
    fw, mw = FOX_WIDTH, MLSTM_WIDTH
    o = 3 * fw + FOX_HEADS
    g0 = o + 3 * mw
    wm = jnp.concatenate([w_in[:, :3 * fw], w_in[:, o:o + 3 * mw], w_in[:, g0 + 2 * MLSTM_HEADS:]],
                         axis=1).astype(BF16)
    wg = jnp.concatenate([w_in[:, 3 * fw:o], w_in[:, g0:g0 + 2 * MLSTM_HEADS]], axis=1)
    wg = jnp.pad(wg, ((0, 0), (0, GATE_LANES - wg.shape[1])))
    bg = jnp.concatenate([b_fox_f, b_mlstm_i, b_mlstm_f])
    bg = jnp.pad(bg, (0, GATE_LANES - bg.shape[0])).reshape(1, GATE_LANES)
    return wm, wg, bg


def kernel(x_prompt, x_sample, c_prompt, c_sample, cache_fox_k, cache_fox_v, cache_fox_logf, state_mlstm_C, state_mlstm_n, state_mlstm_m, page_table, w_ada, b_ada, g_norm_mix, g_norm_ffn, w_in, b_fox_f, b_mlstm_i, b_mlstm_f, g_fox_out, g_mlstm_out, w_out, w_peer_q, peer_keys, peer_u, peer_v, g_final):
    D = D_MODEL
    T = x_prompt.shape[1]
    B, S = x_sample.shape[0], x_sample.shape[1]
    n_pages = page_table.shape[1]
    H, DH = FOX_HEADS, FOX_HEAD_DIM
    MH, DK = MLSTM_HEADS, MLSTM_HEAD_DIM
    SP = 8
    SEG_PER_BLK = 128 // SP

    wm, wg, bg = _prep_weights(w_in[0], b_fox_f[0], b_mlstm_i[0], b_mlstm_f[0])
    w_out_b = w_out[0].astype(BF16)
    w_pq_b = w_peer_q[0].astype(BF16)
    keys_b = peer_keys[0].reshape(2 * PEER_HEADS, PEER_N_KEYS, PEER_HALF).astype(BF16)
    u_b = peer_u[0].astype(BF16)
    vt_b = peer_v[0].T.astype(BF16)
    lane = jnp.arange(FOX_WIDTH)
    bd64 = jnp.where(lane[:, None] // DH == lane[None, :] // DH, 1.0 / DH, 0.0).astype(BF16)
    bd128 = jnp.where(lane[:, None] // DK == lane[None, :] // DK, 1.0 / DK, 0.0).astype(BF16)
    g_mix = g_norm_mix[0].reshape(1, D)
    g_ffn = g_norm_ffn[0].reshape(1, D)
    g_fox = g_fox_out[0].reshape(1, FOX_WIDTH)
    g_ml = g_mlstm_out[0].reshape(1, MLSTM_WIDTH)
    g_fin = g_final.reshape(1, D)

    c_all = jnp.concatenate([c_prompt, c_sample], axis=0)
    c_all = jnp.pad(c_all, ((0, (-c_all.shape[0]) % 8), (0, 0)))
    ada = adaln_terms(c_all, w_ada[0], b_ada[0])

    def terms(a):
        return [a[:, i * D:(i + 1) * D] for i in range(6)]

    shift1_p, scale1_p, gate1_p, shift2_p, scale2_p, gate2_p = terms(ada[0:1])
    shift1_s, scale1_s, gate1_s, shift2_s, scale2_s, gate2_s = terms(
        jnp.repeat(ada[1:1 + B], S, axis=0))

    xp = x_prompt.reshape(T, D)
    fq, fk, fv, mq, mk, mv, mo, gt, gtt = in_projection(xp, scale1_p, shift1_p, g_mix, wm, wg, bg, 256)
    fcum = lane_cumsum(gtt[0:H])

    def heads(a):
        return a.astype(BF16).reshape(T, H, DH).transpose(1, 0, 2)

    o = fox_prompt_attention(heads(fq * DH ** -0.5), heads(fk), heads(fv), fcum.reshape(H, 1, T), 512)
    fox_o = o.transpose(1, 0, 2).reshape(T, FOX_WIDTH)
    m_h, c_p, n_p, m_p = mlstm(
        mq.reshape(1, T, MLSTM_WIDTH), mk.reshape(1, T, MLSTM_WIDTH), mv.reshape(1, T, MLSTM_WIDTH),
        gt.reshape(1, T, GATE_LANES), gtt[0:16].reshape(1, 16, T),
        jnp.zeros((1, MH, DK, DK), F32), jnp.zeros((1, MH, 1, DK), F32), jnp.zeros((1, MH, 1, 1), F32),
        256, 256)
    x1, h2, st = combine(fox_o, m_h.reshape(T, MLSTM_WIDTH), mo, xp, gate1_p, scale2_p, shift2_p,
                         g_fox, g_ml, g_ffn, w_out_b, w_pq_b, keys_b, bd64, bd128, 256)
    u1, u2, th = peer_topk(st)
    y_p = peer_experts(u1, u2, th, h2, u_b, vt_b, x1, gate2_p, g_fin, 512, 1024)

    ns = B * S
    xs = x_sample.reshape(ns, D)
    sfq, sfk, sfv, smq, smk, smv, smo, sgt, _ = in_projection(
        xs, scale1_s, shift1_s, g_mix, wm, wg, bg, 256)
    q4 = (sfq * DH ** -0.5).reshape(B, S, H, 1, DH)
    qbd = (q4 * jnp.eye(H, dtype=F32)[None, None, :, :, None]).astype(BF16).reshape(B, S * H, FOX_WIDTH)

    def pad_tok(a):
        a = a.reshape(B, S, a.shape[-1])
        return jnp.pad(a, ((0, 0), (0, SP - S), (0, 0)))

    lf_new_t = jnp.pad(sgt[:, 0:H].reshape(B, S, H).transpose(0, 2, 1),
                       ((0, 0), (0, 0), (0, PAGE_SIZE - S)))
    n_pool = cache_fox_k.shape[1]
    so = fox_sample_attention(
        page_table, qbd, pad_tok(sfk), pad_tok(sfv), lf_new_t,
        cache_fox_k[0].reshape(n_pool, PAGE_SIZE, FOX_WIDTH),
        cache_fox_v[0].reshape(n_pool, PAGE_SIZE, FOX_WIDTH),
        cache_fox_logf[0].transpose(0, 2, 1), 8)
    pad_gate = jnp.zeros((GATE_LANES,), F32).at[H:H + MH].set(NEG_BIG)
    sg = jnp.concatenate([sgt.reshape(B, S, GATE_LANES),
                          jnp.broadcast_to(pad_gate, (B, SP - S, GATE_LANES))], axis=1)
    sg = sg.reshape(B // SEG_PER_BLK, 128, GATE_LANES)

    def blk(a):
        return pad_tok(a).reshape(B // SEG_PER_BLK, 128, a.shape[-1])

    sm_h, c_s, n_s, m_s = mlstm(
        blk(smq), blk(smk), blk(smv), sg, sg.transpose(0, 2, 1)[:, 0:16],
        state_mlstm_C[0], state_mlstm_n[0].reshape(B, MH, 1, DK), state_mlstm_m[0].reshape(B, MH, 1, 1),
        128, SP)
    sm_h = sm_h.reshape(B, SP, MLSTM_WIDTH)[:, 0:S].reshape(ns, MLSTM_WIDTH)
    sx1, sh2, sst = combine(so.reshape(ns, FOX_WIDTH), sm_h, smo, xs, gate1_s, scale2_s, shift2_s,
                            g_fox, g_ml, g_ffn, w_out_b, w_pq_b, keys_b, bd64, bd128, 256)
    su1, su2, sth = peer_topk(sst)
    y_s = peer_experts(su1, su2, sth, sh2, u_b, vt_b, sx1, gate2_s, g_fin, 512, 1024)

    return (y_p.reshape(1, T, D), y_s.reshape(B, S, D),
            fk.reshape(1, 1, T, H, DH), fv.reshape(1, 1, T, H, DH), gt[:, 0:H].reshape(1, 1, T, H),
            c_p.reshape(1, 1, MH, DK, DK), n_p.reshape(1, 1, MH, DK), m_p.reshape(1, 1, MH),
            sfk.reshape(1, B, S, H, DH), sfv.reshape(1, B, S, H, DH), sgt[:, 0:H].reshape(1, B, S, H),
            c_s.reshape(1, B, MH, DK, DK), n_s.reshape(1, B, MH, DK), m_s.reshape(1, B, MH))
```

```python
import functools
import math

import jax
import jax.numpy as jnp
from jax import lax
from jax.experimental import pallas as pl
from jax.experimental.pallas import tpu as pltpu

F32 = jnp.float32
BF16 = jnp.bfloat16
HIGHEST = lax.Precision.HIGHEST

D_MODEL = 1024
FOX_HEADS = 8
FOX_HEAD_DIM = 64
FOX_WIDTH = FOX_HEADS * FOX_HEAD_DIM
MLSTM_HEADS = 4
MLSTM_HEAD_DIM = 128
MLSTM_WIDTH = MLSTM_HEADS * MLSTM_HEAD_DIM
PAGE_SIZE = 128
PEER_HEADS = 8
PEER_N_KEYS = 128
PEER_TOPK = 16
PEER_HALF = 128
NORM_EPS = 1e-6
GATE_LANES = 128
NEG_BIG = -1e30
LOG2E = 1.4426950408889634
VMEM_LIMIT = 56 * 1024 * 1024


def _cparams(sem):
    return pltpu.CompilerParams(dimension_semantics=sem, vmem_limit_bytes=VMEM_LIMIT)


def _log_sigmoid(x):
    return jnp.minimum(x, 0.0) - jnp.log1p(jnp.exp(-jnp.abs(x)))


def _adaln_kernel(c_ref, w_ref, b_ref, o_ref):
    c = c_ref[...]
    s = c / (1.0 + jnp.exp(-c))
    o_ref[...] = jnp.dot(s, w_ref[...], precision=HIGHEST, preferred_element_type=F32) + b_ref[...]


def adaln_terms(c, w_ada, b_ada):
    rows, d = c.shape
    n = w_ada.shape[1]
    tn = 768
    return pl.pallas_call(
        _adaln_kernel,
        out_shape=jax.ShapeDtypeStruct((rows, n), F32),
        grid=(n // tn,),
        in_specs=[pl.BlockSpec((rows, d), lambda j: (0, 0)),
                  pl.BlockSpec((d, tn), lambda j: (0, j)),
                  pl.BlockSpec((1, tn), lambda j: (0, j))],
        out_specs=pl.BlockSpec((rows, tn), lambda j: (0, j)),
        compiler_params=_cparams(("arbitrary",)),
        name="adaln",
    )(c, w_ada, b_ada.reshape(1, n))


def _inproj_kernel(x_ref, sc_ref, sh_ref, g_ref, wm_ref, wg_ref, bg_ref,
                   fq_ref, fk_ref, fv_ref, mq_ref, mk_ref, mv_ref, mo_ref, gt_ref, gtt_ref):
    x = x_ref[...]
    y = x * lax.rsqrt(jnp.mean(x * x, axis=-1, keepdims=True) + NORM_EPS) * g_ref[...]
    h = y * (1.0 + sc_ref[...]) + sh_ref[...]
    hb = h.astype(BF16)
    for i, o_ref in enumerate((fq_ref, fk_ref, fv_ref, mq_ref, mk_ref, mv_ref, mo_ref)):
        o_ref[...] = jnp.dot(hb, wm_ref[:, i * 512:(i + 1) * 512], preferred_element_type=F32)
    zg = jnp.dot(h, wg_ref[...], precision=HIGHEST, preferred_element_type=F32) + bg_ref[...]
    col = lax.broadcasted_iota(jnp.int32, zg.shape, 1)
    is_i = (col >= FOX_HEADS) & (col < FOX_HEADS + MLSTM_HEADS)
    gt = jnp.where(is_i, zg, _log_sigmoid(zg))
    gt_ref[...] = gt
    gtt_ref[...] = gt.T


def in_projection(x, scale, shift, g, wm, wg, bg, tm):
    n, d = x.shape
    mod_rows = scale.shape[0]
    if mod_rows == 1:
        mod_spec = pl.BlockSpec((1, d), lambda i: (0, 0))
    else:
        mod_spec = pl.BlockSpec((tm, d), lambda i: (i, 0))
    o512 = jax.ShapeDtypeStruct((n, 512), F32)
    s512 = pl.BlockSpec((tm, 512), lambda i: (i, 0))
    return pl.pallas_call(
        _inproj_kernel,
        out_shape=[o512] * 7 + [jax.ShapeDtypeStruct((n, GATE_LANES), F32),
                                jax.ShapeDtypeStruct((GATE_LANES, n), F32)],
        grid=(n // tm,),
        in_specs=[pl.BlockSpec((tm, d), lambda i: (i, 0)), mod_spec, mod_spec,
                  pl.BlockSpec((1, d), lambda i: (0, 0)),
                  pl.BlockSpec(wm.shape, lambda i: (0, 0)),
                  pl.BlockSpec(wg.shape, lambda i: (0, 0)),
                  pl.BlockSpec((1, GATE_LANES), lambda i: (0, 0))],
        out_specs=[s512] * 7 + [pl.BlockSpec((tm, GATE_LANES), lambda i: (i, 0)),
                                pl.BlockSpec((GATE_LANES, tm), lambda i: (0, i))],
        compiler_params=_cparams(("arbitrary",)),
        name="inproj",
    )(x, scale, shift, g, wm, wg, bg)


def _lane_cumsum(x, stride=1):
    n = x.shape[-1]
    lane = lax.broadcasted_iota(jnp.int32, x.shape, x.ndim - 1)
    sh = stride
    while sh < n:
        x = x + jnp.where(lane >= sh, pltpu.roll(x, sh, x.ndim - 1), 0.0)
        sh *= 2
    return x


def _cumsum_kernel(x_ref, f_ref, hi_ref, mid_ref, lo_ref, *, blk):
    f = _lane_cumsum(x_ref[...]) * LOG2E
    f_ref[...] = f
    for b in range(f.shape[-1] // blk):
        cols = slice(b * blk, (b + 1) * blk)
        rel = f[:, cols] - f[:, b * blk:b * blk + 1]
        hi = rel.astype(BF16).astype(F32)
        mid = (rel - hi).astype(BF16).astype(F32)
        hi_ref[:, cols] = hi
        mid_ref[:, cols] = mid
        lo_ref[:, cols] = (rel - hi - mid).astype(BF16).astype(F32)


def forget_cumsum(x, blk):
    out = jax.ShapeDtypeStruct(x.shape, F32)
    return pl.pallas_call(
        functools.partial(_cumsum_kernel, blk=blk),
        out_shape=[out, out, out, out],
        name="cumsum",
    )(x)


def _flash_kernel(qi_ref, ki_ref, fs_ref, qa_ref, ka_ref, vt_ref, o_ref, m_ref, l_ref, acc_ref, *, nb):
    s_idx = pl.program_id(0)
    qi = qi_ref[s_idx]
    ki = ki_ref[s_idx]
    nh = qa_ref.shape[0]

    @pl.when(ki == 0)
    def _():
        m_ref[...] = jnp.full(m_ref.shape, -jnp.inf, F32)
        l_ref[...] = jnp.zeros(l_ref.shape, F32)
        acc_ref[...] = jnp.zeros(acc_ref.shape, F32)

    def step(masked):
        for h in range(nh):
            st = lax.dot_general(ka_ref[h], qa_ref[h], (((1,), (1,)), ((), ())),
                                 preferred_element_type=F32)
            if masked:
                key = lax.broadcasted_iota(jnp.int32, st.shape, 0)
                qry = lax.broadcasted_iota(jnp.int32, st.shape, 1)
                st = jnp.where(key <= qry, st, -jnp.inf)
            cb = -fs_ref[h * nb + ki]
            m_prev = m_ref[h]
            m_new = jnp.maximum(m_prev, jnp.max(st, axis=0, keepdims=True) + cb)
            alpha = jnp.exp2(m_prev - m_new)
            p = jnp.exp2(st + (cb - m_new))
            l_ref[h] = alpha * l_ref[h] + jnp.sum(p, axis=0, keepdims=True)
            acc_ref[h] = alpha * acc_ref[h] + jnp.dot(vt_ref[h], p.astype(BF16),
                                                      preferred_element_type=F32)
            m_ref[h] = m_new

    @pl.when(ki < qi)
    def _():
        step(False)

    @pl.when(ki == qi)
    def _():
        step(True)
        o_ref[...] = acc_ref[...] / l_ref[...]


def fox_prompt_attention(qa, ka, vt, f_start, t_blk):
    h, t, da = qa.shape
    dh = vt.shape[1]
    nb = t // t_blk
    pairs = [(i, j) for i in range(nb) for j in range(i + 1)]
    qi_arr = jnp.array([p[0] for p in pairs], jnp.int32)
    ki_arr = jnp.array([p[1] for p in pairs], jnp.int32)
    grid_spec = pltpu.PrefetchScalarGridSpec(
        num_scalar_prefetch=3,
        grid=(len(pairs),),
        in_specs=[pl.BlockSpec((h, t_blk, da), lambda s, qi, ki, fs: (0, qi[s], 0)),
                  pl.BlockSpec((h, t_blk, da), lambda s, qi, ki, fs: (0, ki[s], 0)),
                  pl.BlockSpec((h, dh, t_blk), lambda s, qi, ki, fs: (0, 0, ki[s]))],
        out_specs=pl.BlockSpec((h, dh, t_blk), lambda s, qi, ki, fs: (0, 0, qi[s])),
        scratch_shapes=[pltpu.VMEM((h, 1, t_blk), F32), pltpu.VMEM((h, 1, t_blk), F32),
                        pltpu.VMEM((h, dh, t_blk), F32)],
    )
    return pl.pallas_call(
        functools.partial(_flash_kernel, nb=nb),
        out_shape=jax.ShapeDtypeStruct((h, dh, t), F32),
        grid_spec=grid_spec,
        compiler_params=_cparams(("arbitrary",)),
        name="fox_prompt",
    )(qi_arr, ki_arr, f_start, qa, ka, vt)


def _mlstm_kernel(q_ref, k_ref, v_ref, g_ref, gt_ref, c0_ref, n0_ref, m0_ref,
                  h_ref, c_out_ref, n_out_ref, m_out_ref, c_s, n_s, m_s, *, rows, seg_len):
    nseg = rows // seg_len
    dk = MLSTM_HEAD_DIM
    c_idx = pl.program_id(1)

    @pl.when(c_idx == 0)
    def _():
        c_s[...] = c0_ref[...]
        n_s[...] = n0_ref[...]
        m_s[...] = m0_ref[...]

    g = g_ref[0]
    gt = gt_ref[0]
    row = lax.broadcasted_iota(jnp.int32, (rows, rows), 0)
    col = lax.broadcasted_iota(jnp.int32, (rows, rows), 1)
    if nseg == 1:
        causal = col <= row
        anti = row <= col
    else:
        same = (row // seg_len) == (col // seg_len)
        causal = (col <= row) & same
        anti = (row <= col) & same
    rid = lax.broadcasted_iota(jnp.int32, (rows, 1), 0) // seg_len

    def per_row(vals):
        out = jnp.broadcast_to(vals[0], (rows, 1))
        for j in range(1, nseg):
            out = jnp.where(rid == j, vals[j], out)
        return out

    for h in range(MLSTM_HEADS):
        lane = slice(h * dk, (h + 1) * dk)
        q = q_ref[0, :, lane]
        k = k_ref[0, :, lane] * (dk ** -0.5)
        v = v_ref[0, :, lane]
        i_row = gt[8 + h:9 + h, :]
        lf_row = gt[12 + h:13 + h, :]
        i_col = g[:, 8 + h:9 + h]
        lf_col = g[:, 12 + h:13 + h]
        b_col = jnp.sum(jnp.where(causal, lf_row, 0.0), axis=1, keepdims=True)
        b_row = jnp.sum(jnp.where(anti, lf_col, 0.0), axis=0, keepdims=True)
        log_d = jnp.where(causal, b_col - b_row + i_row, -jnp.inf)
        m_prev = [m_s[j, h] for j in range(nseg)]
        m_inter = b_col + per_row(m_prev)
        m_t = jnp.maximum(m_inter, jnp.max(log_d, axis=1, keepdims=True))
        d_mat = jnp.exp(log_d - m_t)
        inter = jnp.exp(m_inter - m_t)
        qb = q.astype(BF16)
        s_mat = lax.dot_general(qb, k.astype(BF16), (((1,), (1,)), ((), ())),
                                preferred_element_type=F32) * d_mat
        num = jnp.dot(s_mat.astype(BF16), v.astype(BF16), preferred_element_type=F32)
        qc = jnp.concatenate(
            [jnp.dot(qb[j * seg_len:(j + 1) * seg_len], c_s[j, h].astype(BF16),
                     preferred_element_type=F32) for j in range(nseg)], axis=0)
        n_rows = jnp.concatenate(
            [jnp.broadcast_to(n_s[j, h], (seg_len, dk)) for j in range(nseg)], axis=0)
        qn = jnp.sum(q * n_rows, axis=1, keepdims=True)
        num = inter * qc + num
        den = inter * qn + jnp.sum(s_mat, axis=1, keepdims=True)
        h_ref[0, :, lane] = num / jnp.maximum(jnp.abs(den), jnp.exp(-m_t))

        last = [(j + 1) * seg_len - 1 for j in range(nseg)]
        b_last = [b_col[r:r + 1, :] for r in last]
        m_new = [m_t[r:r + 1, :] for r in last]
        w_end = jnp.exp(per_row(b_last) - b_col + i_col - per_row(m_new))
        kw = k * w_end
        kwt = kw.T.astype(BF16)
        vb = v.astype(BF16)
        if nseg > 1:
            vb = jnp.concatenate([jnp.where(rid == j, vb, jnp.zeros_like(vb))
                                  for j in range(nseg)], axis=1)
        upd = jnp.dot(kwt, vb, preferred_element_type=F32)
        for j in range(nseg):
            decay = jnp.exp(b_last[j] + m_prev[j] - m_new[j])
            kw_j = kw if nseg == 1 else jnp.where(rid == j, kw, 0.0)
            c_s[j, h] = decay * c_s[j, h] + upd[:, j * dk:(j + 1) * dk]
            n_s[j, h] = decay * n_s[j, h] + jnp.sum(kw_j, axis=0, keepdims=True)
            m_s[j, h] = m_new[j]

    @pl.when(c_idx == pl.num_programs(1) - 1)
    def _():
        c_out_ref[...] = c_s[...]
        n_out_ref[...] = n_s[...]
        m_out_ref[...] = m_s[...]


def mlstm(q, k, v, gates, gates_t, c0, n0, m0, rows, seg_len):
    G, tg, w = q.shape
    nseg = rows // seg_len
    nc = tg // rows
    assert nseg == 1 or nc == 1
    qspec = pl.BlockSpec((1, rows, w), lambda gi, c: (gi, c, 0))
    cspec = pl.BlockSpec((nseg, MLSTM_HEADS, 128, 128), lambda gi, c: (gi, 0, 0, 0))
    nspec = pl.BlockSpec((nseg, MLSTM_HEADS, 1, 128), lambda gi, c: (gi, 0, 0, 0))
    mspec = pl.BlockSpec((nseg, MLSTM_HEADS, 1, 1), lambda gi, c: (gi, 0, 0, 0))
    return pl.pallas_call(
        functools.partial(_mlstm_kernel, rows=rows, seg_len=seg_len),
        out_shape=[jax.ShapeDtypeStruct((G, tg, w), F32),
                   jax.ShapeDtypeStruct(c0.shape, F32),
                   jax.ShapeDtypeStruct(n0.shape, F32),
                   jax.ShapeDtypeStruct(m0.shape, F32)],
        grid=(G, nc),
        in_specs=[qspec, qspec, qspec,
                  pl.BlockSpec((1, rows, GATE_LANES), lambda gi, c: (gi, c, 0)),
                  pl.BlockSpec((1, 16, rows), lambda gi, c: (gi, 0, c)),
                  cspec, nspec, mspec],
        out_specs=[qspec, cspec, nspec, mspec],
        scratch_shapes=[pltpu.VMEM((nseg, MLSTM_HEADS, 128, 128), F32),
                        pltpu.VMEM((nseg, MLSTM_HEADS, 1, 128), F32),
                        pltpu.VMEM((nseg, MLSTM_HEADS, 1, 1), F32)],
        compiler_params=_cparams(("arbitrary", "arbitrary")),
        name="mlstm",
    )(q, k, v, gates, gates_t, c0, n0, m0)


def _fox_sample_kernel(pt_ref, qbd_ref, knew_ref, vnew_ref, lfnew_ref, *rest, pps):
    k_refs = rest[0:pps]
    v_refs = rest[pps:2 * pps]
    lf_refs = rest[2 * pps:3 * pps]
    o_ref = rest[3 * pps]
    m_ref, l_ref, acc_ref, f_ref = rest[3 * pps + 1:]
    step = pl.program_id(1)
    nq = qbd_ref.shape[1]
    nh = FOX_HEADS
    page_keys = PAGE_SIZE * nh

    @pl.when(step == 0)
    def _():
        m_ref[...] = jnp.full(m_ref.shape, -jnp.inf, F32)
        l_ref[...] = jnp.zeros(l_ref.shape, F32)
        acc_ref[...] = jnp.zeros(acc_ref.shape, F32)
        f_ref[...] = jnp.zeros(f_ref.shape, F32)

    q = qbd_ref[0]

    def scores(k_rows, f_row, causal):
        s = lax.dot_general(q, k_rows.astype(BF16), (((1,), (1,)), ((), ())),
                            preferred_element_type=F32)
        row = lax.broadcasted_iota(jnp.int32, s.shape, 0)
        col = lax.broadcasted_iota(jnp.int32, s.shape, 1)
        ok = (col % nh) == (row % nh)
        if causal:
            ok = ok & (col // nh <= row // nh)
        return jnp.where(ok, s - f_row, -jnp.inf)

    def update(logits, v_rows):
        m_prev = m_ref[...]
        m_new = jnp.maximum(m_prev, jnp.max(logits, axis=-1, keepdims=True))
        alpha = jnp.exp(m_prev - m_new)
        p = jnp.exp(logits - m_new)
        l_ref[...] = alpha * l_ref[...] + jnp.sum(p, axis=-1, keepdims=True)
        pb = p.astype(BF16)
        nk = v_rows[0].shape[0]
        pv = jnp.dot(pb[:, 0:nk], v_rows[0].astype(BF16), preferred_element_type=F32)
        for i in range(1, len(v_rows)):
            pv += jnp.dot(pb[:, i * nk:(i + 1) * nk], v_rows[i].astype(BF16),
                          preferred_element_type=F32)
        acc_ref[...] = alpha * acc_ref[...] + pv
        m_ref[...] = m_new

    def head_totals(x):
        n = x.shape[-1]
        lane = lax.broadcasted_iota(jnp.int32, x.shape, x.ndim - 1)
        t = jnp.where(lane >= n - nh, x, 0.0)
        sh = nh
        while sh < n:
            t = t + pltpu.roll(t, n - sh, x.ndim - 1)
            sh *= 2
        return t

    lf = jnp.concatenate([r[0] for r in lf_refs], axis=0)
    within = _lane_cumsum(lf, stride=nh)
    tot = head_totals(within)
    sub = lax.broadcasted_iota(jnp.int32, tot.shape, 0)
    incl = tot
    sh = 1
    while sh < pps:
        incl = incl + jnp.where(sub >= sh, pltpu.roll(incl, sh, 0), 0.0)
        sh *= 2
    carry = f_ref[...]
    f_pages = within + (incl - tot) + carry
    carry = carry + incl[pps - 1:pps, :]
    f_ref[...] = carry
    logits = [scores(k_refs[i][0, 0].reshape(page_keys, FOX_HEAD_DIM), f_pages[i:i + 1, :], False)
              for i in range(pps)]
    update(jnp.concatenate(logits, axis=1),
           [r[0, 0].reshape(page_keys, FOX_HEAD_DIM) for r in v_refs])

    @pl.when(step == pl.num_programs(1) - 1)
    def _():
        f_new = _lane_cumsum(lfnew_ref[0], stride=nh) + carry[:, 0:lfnew_ref.shape[2]]
        update(scores(knew_ref[0], f_new, True), [vnew_ref[0]])
        o_ref[0] = acc_ref[...] / l_ref[...]


def fox_sample_attention(page_table, q_rows, k_new, v_new, lf_new, k_cache, v_cache, lf_pool, pps):
    b, nq, dh = q_rows.shape
    n_pages = page_table.shape[1]
    pt = page_table.reshape(-1)
    page_keys = PAGE_SIZE * FOX_HEADS

    def page_map(i):
        return lambda bi, s, pt_ref: (0, pt_ref[bi * n_pages + s * pps + i], 0, 0, 0)

    def lf_map(i):
        return lambda bi, s, pt_ref: (pt_ref[bi * n_pages + s * pps + i], 0, 0)

    def bmap(bi, s, pt_ref):
        return (bi, 0, 0)

    page_blk = (1, 1, PAGE_SIZE, FOX_HEADS, dh)
    in_specs = [pl.BlockSpec((1, nq, dh), bmap),
                pl.BlockSpec((1,) + k_new.shape[1:], bmap),
                pl.BlockSpec((1,) + v_new.shape[1:], bmap),
                pl.BlockSpec((1,) + lf_new.shape[1:], bmap)]
    in_specs += [pl.BlockSpec(page_blk, page_map(i)) for i in range(pps)]
    in_specs += [pl.BlockSpec(page_blk, page_map(i)) for i in range(pps)]
    in_specs += [pl.BlockSpec((1, 1, page_keys), lf_map(i)) for i in range(pps)]
    grid_spec = pltpu.PrefetchScalarGridSpec(
        num_scalar_prefetch=1,
        grid=(b, n_pages // pps),
        in_specs=in_specs,
        out_specs=pl.BlockSpec((1, nq, dh), bmap),
        scratch_shapes=[pltpu.VMEM((nq, 1), F32), pltpu.VMEM((nq, 1), F32),
                        pltpu.VMEM((nq, dh), F32), pltpu.VMEM((1, page_keys), F32)],
    )
    return pl.pallas_call(
        functools.partial(_fox_sample_kernel, pps=pps),
        out_shape=jax.ShapeDtypeStruct((b, nq, dh), F32),
        grid_spec=grid_spec,
        compiler_params=_cparams(("arbitrary", "arbitrary")),
        name="fox_sample",
    )(pt, q_rows, k_new, v_new, lf_new, *([k_cache] * pps), *([v_cache] * pps), *([lf_pool] * pps))


def _group_mean_sq(x, bd):
    sq = x * x
    hi = sq.astype(BF16)
    lo = (sq - hi.astype(F32)).astype(BF16)
    return (jnp.dot(hi, bd, preferred_element_type=F32) + jnp.dot(lo, bd, preferred_element_type=F32))


def _combine_kernel(fo_ref, mh_ref, mo_ref, x_ref, g1_ref, sc2_ref, sh2_ref, gfo_ref, gml_ref,
                    gffn_ref, wout_ref, wpq_ref, keys_ref, bd64_ref, bd128_ref,
                    x1_ref, h2_ref, st_ref):
    fo = fo_ref[...]
    a_out = fo * lax.rsqrt(_group_mean_sq(fo, bd64_ref[...]) + NORM_EPS) * gfo_ref[...]
    mh = mh_ref[...]
    mo = mo_ref[...]
    b_out = (mh * lax.rsqrt(_group_mean_sq(mh, bd128_ref[...]) + NORM_EPS) * gml_ref[...]
             / (1.0 + jnp.exp(-mo)))
    cat = jnp.concatenate([a_out, b_out], axis=1).astype(BF16)
    x1 = x_ref[...] + g1_ref[...] * jnp.dot(cat, wout_ref[...], preferred_element_type=F32)
    x1_ref[...] = x1
    y = x1 * lax.rsqrt(jnp.mean(x1 * x1, axis=-1, keepdims=True) + NORM_EPS) * gffn_ref[...]
    h2 = (y * (1.0 + sc2_ref[...]) + sh2_ref[...]).astype(BF16)
    h2_ref[...] = h2
    pq = jnp.dot(h2, wpq_ref[...], preferred_element_type=F32).astype(BF16)
    for hp in range(2 * PEER_HEADS):
        st_ref[hp] = lax.dot_general(keys_ref[hp], pq[:, hp * PEER_HALF:(hp + 1) * PEER_HALF],
                                     (((1,), (1,)), ((), ())), preferred_element_type=F32)


def combine(fox_o, m_h, mo, x, gate1, scale2, shift2, g_fox, g_ml, g_ffn, w_out, w_pq, keys,
            bd64, bd128, tm):
    n, d = x.shape
    mod_rows = gate1.shape[0]
    if mod_rows == 1:
        mod_spec = pl.BlockSpec((1, d), lambda i: (0, 0))
    else:
        mod_spec = pl.BlockSpec((tm, d), lambda i: (i, 0))

    def full(a):
        return pl.BlockSpec(a.shape, lambda i: (0,) * a.ndim)

    s512 = pl.BlockSpec((tm, 512), lambda i: (i, 0))
    sd = pl.BlockSpec((tm, d), lambda i: (i, 0))
    return pl.pallas_call(
        _combine_kernel,
        out_shape=[jax.ShapeDtypeStruct((n, d), F32), jax.ShapeDtypeStruct((n, d), BF16),
                   jax.ShapeDtypeStruct((2 * PEER_HEADS, PEER_N_KEYS, n), F32)],
        grid=(n // tm,),
        in_specs=[s512, s512, s512, sd, mod_spec, mod_spec, mod_spec, full(g_fox), full(g_ml),
                  full(g_ffn), full(w_out), full(w_pq), full(keys), full(bd64), full(bd128)],
        out_specs=[sd, sd, pl.BlockSpec((2 * PEER_HEADS, PEER_N_KEYS, tm), lambda i: (0, 0, i))],
        compiler_params=_cparams(("arbitrary",)),
        name="combine",
    )(fox_o, m_h, mo, x, gate1, scale2, shift2, g_fox, g_ml, g_ffn, w_out, w_pq, keys, bd64, bd128)


def _sort_desc(a):
    a = list(a)
    n = len(a)
    k = 2
    while k <= n:
        j = k // 2
        while j >= 1:
            for i in range(n):
                p = i ^ j
                if p > i:
                    hi, lo = jnp.maximum(a[i], a[p]), jnp.minimum(a[i], a[p])
                    a[i], a[p] = (hi, lo) if (i & k) == 0 else (lo, hi)
            j //= 2
        k *= 2
    return a


def _merge_desc(a):
    a = list(a)
    j = len(a) // 2
    while j >= 1:
        for i in range(len(a)):
            p = i ^ j
            if p > i:
                a[i], a[p] = jnp.maximum(a[i], a[p]), jnp.minimum(a[i], a[p])
        j //= 2
    return a


_PEER_CAND = [(i, j) for i in range(PEER_TOPK) for j in range(PEER_TOPK)
              if (i + 1) * (j + 1) <= PEER_TOPK]


def _top16_of_128(slabs):
    t = _sort_desc(slabs)
    sub = lax.broadcasted_iota(jnp.int32, t[0].shape, 0)
    for bit in (1, 2, 4):
        if bit == 4:
            partner = [pltpu.roll(v, 4, 0) for v in t]
        else:
            low = (sub & bit) == 0
            partner = [jnp.where(low, pltpu.roll(v, 8 - bit, 0), pltpu.roll(v, bit, 0)) for v in t]
        t = _merge_desc([jnp.maximum(t[i], partner[PEER_TOPK - 1 - i]) for i in range(PEER_TOPK)])
    return t


def _kth_candidate_sums(t1, t2):
    c = [t1[i] + t2[j] for (i, j) in _PEER_CAND]
    c += [jnp.full(c[0].shape, NEG_BIG, F32)] * (64 - len(c))
    return _sort_desc(c)


def _topk_kernel(st_ref, u1_ref, u2_ref, th_ref, t1_s, t2_s):
    nslab = PEER_N_KEYS // 8

    def pack(t, t_s, h):
        for i in range(PEER_TOPK):
            t_s[i, h:h + 1, :] = t[i][0:1, :]

    for h in range(PEER_HEADS):
        u1_ref[h] = st_ref[2 * h] * LOG2E
        u2_ref[h] = st_ref[2 * h + 1] * LOG2E
        pack(_top16_of_128([u1_ref[h, 8 * i:8 * i + 8, :] for i in range(nslab)]), t1_s, h)
        pack(_top16_of_128([u2_ref[h, 8 * i:8 * i + 8, :] for i in range(nslab)]), t2_s, h)
    t1 = [t1_s[i] for i in range(PEER_TOPK)]
    c = _kth_candidate_sums(t1, [t2_s[i] for i in range(PEER_TOPK)])
    z = jnp.exp2(c[0] - c[0])
    for r in range(1, PEER_TOPK):
        z = z + jnp.exp2(c[r] - c[0])
    k2 = c[0] + jnp.log2(z)
    for h in range(PEER_HEADS):
        u2_ref[h] = u2_ref[h] - k2[h:h + 1, :]
        pack(_top16_of_128([u2_ref[h, 8 * i:8 * i + 8, :] for i in range(nslab)]), t2_s, h)
    c = _kth_candidate_sums(t1, [t2_s[i] for i in range(PEER_TOPK)])
    th_ref[...] = c[PEER_TOPK - 1]


def peer_topk(st):
    tm = 128
    hp, nk, n = st.shape
    uspec = pl.BlockSpec((PEER_HEADS, nk, tm), lambda i: (0, 0, i))
    ushape = jax.ShapeDtypeStruct((PEER_HEADS, nk, n), F32)
    return pl.pallas_call(
        _topk_kernel,
        out_shape=[ushape, ushape, jax.ShapeDtypeStruct((PEER_HEADS, n), F32)],
        grid=(n // tm,),
        in_specs=[pl.BlockSpec((hp, nk, tm), lambda i: (0, 0, i))],
        out_specs=[uspec, uspec, pl.BlockSpec((PEER_HEADS, tm), lambda i: (0, i))],
        scratch_shapes=[pltpu.VMEM((PEER_TOPK, 8, 128), F32), pltpu.VMEM((PEER_TOPK, 8, 128), F32)],
        compiler_params=_cparams(("arbitrary",)),
        name="peer_topk",
    )(st)


def _gelu(x):
    return 0.5 * x * (1.0 + lax.erf(x * (2.0 ** -0.5)))


def _peer_kernel(u1_ref, u2_ref, th_ref, h2_ref, u_ref, vt_ref, x1_ref, g2_ref, gf_ref, o_ref,
                 acc_ref, at_ref, p_ref, *, tm, te):
    e = pl.program_id(1)

    @pl.when(e == 0)
    def _():
        acc_ref[...] = jnp.zeros(acc_ref.shape, F32)

    at_ref[...] = lax.dot_general(u_ref[...], h2_ref[...], (((1,), (1,)), ((), ())),
                                  preferred_element_type=F32)
    a_per_blk = te // PEER_N_KEYS

    heads = range(PEER_HEADS)

    def weighted(r8, th8, rows, lanes):
        w = None
        for h in heads:
            c = r8[h] + u2_ref[h, rows.start % PEER_N_KEYS:rows.start % PEER_N_KEYS + 8, lanes]
            e = jnp.where(c >= th8[h], jnp.exp2(c), 0.0)
            w = e if w is None else w + e
        return w * _gelu(at_ref[rows, lanes])

    for tg in range(tm // 128):
        lanes = slice(tg * 128, (tg + 1) * 128)
        th8 = [jnp.broadcast_to(th_ref[h:h + 1, lanes], (8, 128)) for h in heads]
        for al in range(a_per_blk):
            r8 = [jnp.broadcast_to(u1_ref[h, al:al + 1, lanes], (8, 128)) for h in heads]
            for i in range(0, PEER_N_KEYS, 16):
                r0 = al * PEER_N_KEYS + i
                pair = [weighted(r8, th8, slice(r0 + d, r0 + d + 8), lanes) for d in (0, 8)]
                p_ref[r0:r0 + 16, lanes] = jnp.concatenate(pair, axis=0).astype(BF16)
    acc_ref[...] += jnp.dot(vt_ref[...], p_ref[...], preferred_element_type=F32)

    @pl.when(e == pl.num_programs(1) - 1)
    def _():
        xo = x1_ref[...] + g2_ref[...] * acc_ref[...].T
        o_ref[...] = xo * lax.rsqrt(jnp.mean(xo * xo, axis=-1, keepdims=True) + NORM_EPS) * gf_ref[...]


def peer_experts(u1, u2, th, h2, u_bf, vt_bf, x1, gate2, g_final, tm, te):
    n, d = x1.shape
    n_exp = u_bf.shape[0]
    mod_rows = gate2.shape[0]
    if mod_rows == 1:
        mod_spec = pl.BlockSpec((1, d), lambda i, e: (0, 0))
    else:
        mod_spec = pl.BlockSpec((tm, d), lambda i, e: (i, 0))
    uspec = pl.BlockSpec((PEER_HEADS, PEER_N_KEYS, tm), lambda i, e: (0, 0, i))
    u1spec = pl.BlockSpec((PEER_HEADS, te // PEER_N_KEYS, tm), lambda i, e: (0, e, i))
    return pl.pallas_call(
        functools.partial(_peer_kernel, tm=tm, te=te),
        out_shape=jax.ShapeDtypeStruct((n, d), F32),
        grid=(n // tm, n_exp // te),
        in_specs=[u1spec, uspec, pl.BlockSpec((PEER_HEADS, tm), lambda i, e: (0, i)),
                  pl.BlockSpec((tm, d), lambda i, e: (i, 0)),
                  pl.BlockSpec((te, d), lambda i, e: (e, 0)),
                  pl.BlockSpec((d, te), lambda i, e: (0, e)),
                  pl.BlockSpec((tm, d), lambda i, e: (i, 0)), mod_spec,
                  pl.BlockSpec((1, d), lambda i, e: (0, 0))],
        out_specs=pl.BlockSpec((tm, d), lambda i, e: (i, 0)),
        scratch_shapes=[pltpu.VMEM((d, tm), F32), pltpu.VMEM((te, tm), F32),
                        pltpu.VMEM((te, tm), BF16)],
        compiler_params=_cparams(("arbitrary", "arbitrary")),
        name="peer_experts",
    )(u1, u2, th, h2, u_bf, vt_bf, x1, gate2, g_final)


def _fox_prompt(fq, fk, fv, logf_t, t_blk):
    t = fq.shape[0]
    nh, dh = FOX_HEADS, FOX_HEAD_DIM
    f2, hi, mid, lo = forget_cumsum(logf_t, t_blk)

    def heads(a):
        return a.astype(BF16).reshape(t, nh, dh).transpose(1, 0, 2)

    aug = 128 - dh - 3
    qa = jnp.concatenate([heads(fq * (dh ** -0.5 * LOG2E)), jnp.full((nh, t, 3), -1.0, BF16),
                          jnp.zeros((nh, t, aug), BF16)], axis=-1)
    ka = jnp.concatenate([heads(fk), jnp.stack([hi, mid, lo], axis=-1).astype(BF16),
                          jnp.zeros((nh, t, aug), BF16)], axis=-1)
    vt = fv.astype(BF16).reshape(t, nh, dh).transpose(1, 2, 0)
    o_t = fox_prompt_attention(qa, ka, vt, f2[:, ::t_blk].reshape(-1), t_blk)
    return o_t.reshape(nh * dh, t).T


def _prep_weights(w_in, b_fox_f, b_mlstm_i, b_mlstm_f):
    fw, mw = FOX_WIDTH, MLSTM_WIDTH
    o = 3 * fw + FOX_HEADS
    g0 = o + 3 * mw
    wm = jnp.concatenate([w_in[:, :3 * fw], w_in[:, o:o + 3 * mw], w_in[:, g0 + 2 * MLSTM_HEADS:]],
                         axis=1).astype(BF16)
    wg = jnp.concatenate([w_in[:, 3 * fw:o], w_in[:, g0:g0 + 2 * MLSTM_HEADS]], axis=1)
    wg = jnp.pad(wg, ((0, 0), (0, GATE_LANES - wg.shape[1])))
    bg = jnp.concatenate([b_fox_f, b_mlstm_i, b_mlstm_f])
    bg = jnp.pad(bg, (0, GATE_LANES - bg.shape[0])).reshape(1, GATE_LANES)
    return wm, wg, bg


def kernel(x_prompt, x_sample, c_prompt, c_sample, cache_fox_k, cache_fox_v, cache_fox_logf, state_mlstm_C, state_mlstm_n, state_mlstm_m, page_table, w_ada, b_ada, g_norm_mix, g_norm_ffn, w_in, b_fox_f, b_mlstm_i, b_mlstm_f, g_fox_out, g_mlstm_out, w_out, w_peer_q, peer_keys, peer_u, peer_v, g_final):
    D = D_MODEL
    T = x_prompt.shape[1]
    B, S = x_sample.shape[0], x_sample.shape[1]
    n_pages = page_table.shape[1]
    H, DH = FOX_HEADS, FOX_HEAD_DIM
    MH, DK = MLSTM_HEADS, MLSTM_HEAD_DIM
    SP = 8
    SEG_PER_BLK = 128 // SP

    wm, wg, bg = _prep_weights(w_in[0], b_fox_f[0], b_mlstm_i[0], b_mlstm_f[0])
    w_out_b = w_out[0].astype(BF16)
    w_pq_b = w_peer_q[0].astype(BF16)
    keys_b = peer_keys[0].reshape(2 * PEER_HEADS, PEER_N_KEYS, PEER_HALF).astype(BF16)
    u_b = peer_u[0].astype(BF16)
    vt_b = peer_v[0].T.astype(BF16)
    lane = jnp.arange(FOX_WIDTH)
    bd64 = jnp.where(lane[:, None] // DH == lane[None, :] // DH, 1.0 / DH, 0.0).astype(BF16)
    bd128 = jnp.where(lane[:, None] // DK == lane[None, :] // DK, 1.0 / DK, 0.0).astype(BF16)
    g_mix = g_norm_mix[0].reshape(1, D)
    g_ffn = g_norm_ffn[0].reshape(1, D)
    g_fox = g_fox_out[0].reshape(1, FOX_WIDTH)
    g_ml = g_mlstm_out[0].reshape(1, MLSTM_WIDTH)
    g_fin = g_final.reshape(1, D)

    c_all = jnp.concatenate([c_prompt, c_sample], axis=0)
    c_all = jnp.pad(c_all, ((0, (-c_all.shape[0]) % 8), (0, 0)))
    ada = adaln_terms(c_all, w_ada[0], b_ada[0])

    def terms(a):
        return [a[:, i * D:(i + 1) * D] for i in range(6)]

    shift1_p, scale1_p, gate1_p, shift2_p, scale2_p, gate2_p = terms(ada[0:1])
    shift1_s, scale1_s, gate1_s, shift2_s, scale2_s, gate2_s = terms(
        jnp.repeat(ada[1:1 + B], S, axis=0))

    xp = x_prompt.reshape(T, D)
    fq, fk, fv, mq, mk, mv, mo, gt, gtt = in_projection(xp, scale1_p, shift1_p, g_mix, wm, wg, bg, 256)
    fox_o = _fox_prompt(fq, fk, fv, gtt[0:H], 512)
    m_h, c_p, n_p, m_p = mlstm(
        mq.reshape(1, T, MLSTM_WIDTH), mk.reshape(1, T, MLSTM_WIDTH), mv.reshape(1, T, MLSTM_WIDTH),
        gt.reshape(1, T, GATE_LANES), gtt[0:16].reshape(1, 16, T),
        jnp.zeros((1, MH, DK, DK), F32), jnp.zeros((1, MH, 1, DK), F32), jnp.zeros((1, MH, 1, 1), F32),
        256, 256)
    x1, h2, st = combine(fox_o, m_h.reshape(T, MLSTM_WIDTH), mo, xp, gate1_p, scale2_p, shift2_p,
                         g_fox, g_ml, g_ffn, w_out_b, w_pq_b, keys_b, bd64, bd128, 256)
    u1, u2, th = peer_topk(st)
    y_p = peer_experts(u1, u2, th, h2, u_b, vt_b, x1, gate2_p, g_fin, 512, 1024)

    ns = B * S
    xs = x_sample.reshape(ns, D)
    sfq, sfk, sfv, smq, smk, smv, smo, sgt, _ = in_projection(
        xs, scale1_s, shift1_s, g_mix, wm, wg, bg, 256)
    def pad_tok(a):
        a = a.reshape(B, S, a.shape[-1])
        return jnp.pad(a, ((0, 0), (0, SP - S), (0, 0)))

    def key_rows(a):
        return jnp.pad(a.reshape(B, S * H, DH), ((0, 0), (0, PAGE_SIZE - S * H), (0, 0)))

    n_pool = cache_fox_k.shape[1]
    so = fox_sample_attention(
        page_table, (sfq * DH ** -0.5).astype(BF16).reshape(B, S * H, DH), key_rows(sfk), key_rows(sfv),
        jnp.pad(sgt[:, 0:H].reshape(B, 1, S * H), ((0, 0), (0, 0), (0, PAGE_SIZE - S * H))),
        cache_fox_k, cache_fox_v, cache_fox_logf[0].reshape(n_pool, 1, PAGE_SIZE * H), 8)
    pad_gate = jnp.zeros((GATE_LANES,), F32).at[H:H + MH].set(NEG_BIG)
    sg = jnp.concatenate([sgt.reshape(B, S, GATE_LANES),
                          jnp.broadcast_to(pad_gate, (B, SP - S, GATE_LANES))], axis=1)
    sg = sg.reshape(B // SEG_PER_BLK, 128, GATE_LANES)

    def blk(a):
        return pad_tok(a).reshape(B // SEG_PER_BLK, 128, a.shape[-1])

    sm_h, c_s, n_s, m_s = mlstm(
        blk(smq), blk(smk), blk(smv), sg, sg.transpose(0, 2, 1)[:, 0:16],
        state_mlstm_C[0], state_mlstm_n[0].reshape(B, MH, 1, DK), state_mlstm_m[0].reshape(B, MH, 1, 1),
        128, SP)
    sm_h = sm_h.reshape(B, SP, MLSTM_WIDTH)[:, 0:S].reshape(ns, MLSTM_WIDTH)
    sx1, sh2, sst = combine(so.reshape(ns, FOX_WIDTH), sm_h, smo, xs, gate1_s, scale2_s, shift2_s,
                            g_fox, g_ml, g_ffn, w_out_b, w_pq_b, keys_b, bd64, bd128, 256)
    su1, su2, sth = peer_topk(sst)
    y_s = peer_experts(su1, su2, sth, sh2, u_b, vt_b, sx1, gate2_s, g_fin, 512, 1024)

    return (y_p.reshape(1, T, D), y_s.reshape(B, S, D),
            fk.reshape(1, 1, T, H, DH), fv.reshape(1, 1, T, H, DH), gt[:, 0:H].reshape(1, 1, T, H),
            c_p.reshape(1, 1, MH, DK, DK), n_p.reshape(1, 1, MH, DK), m_p.reshape(1, 1, MH),
            sfk.reshape(1, B, S, H, DH), sfv.reshape(1, B, S, H, DH), sgt[:, 0:H].reshape(1, B, S, H),
            c_s.reshape(1, B, MH, DK, DK), n_s.reshape(1, B, MH, DK), m_s.reshape(1, B, MH))
```

```python
import functools
import math

import jax
import jax.numpy as jnp
from jax import lax
from jax.experimental import pallas as pl
from jax.experimental.pallas import tpu as pltpu

F32 = jnp.float32
BF16 = jnp.bfloat16
HIGHEST = lax.Precision.HIGHEST

D_MODEL = 1024
FOX_HEADS = 8
FOX_HEAD_DIM = 64
FOX_WIDTH = FOX_HEADS * FOX_HEAD_DIM
MLSTM_HEADS = 4
MLSTM_HEAD_DIM = 128
MLSTM_WIDTH = MLSTM_HEADS * MLSTM_HEAD_DIM
PAGE_SIZE = 128
PEER_HEADS = 8
PEER_N_KEYS = 128
PEER_TOPK = 16
PEER_HALF = 128
NORM_EPS = 1e-6
GATE_LANES = 128
NEG_BIG = -1e30
LOG2E = 1.4426950408889634
VMEM_LIMIT = 56 * 1024 * 1024


def _cparams(sem):
    return pltpu.CompilerParams(dimension_semantics=sem, vmem_limit_bytes=VMEM_LIMIT)


def _log_sigmoid(x):
    return jnp.minimum(x, 0.0) - jnp.log1p(jnp.exp(-jnp.abs(x)))


def _adaln_kernel(c_ref, w_ref, b_ref, o_ref):
    c = c_ref[...]
    s = c / (1.0 + jnp.exp(-c))
    o_ref[...] = jnp.dot(s, w_ref[...], precision=HIGHEST, preferred_element_type=F32) + b_ref[...]


def adaln_terms(c, w_ada, b_ada):
    rows, d = c.shape
    n = w_ada.shape[1]
    tn = 768
    return pl.pallas_call(
        _adaln_kernel,
        out_shape=jax.ShapeDtypeStruct((rows, n), F32),
        grid=(n // tn,),
        in_specs=[pl.BlockSpec((rows, d), lambda j: (0, 0)),
                  pl.BlockSpec((d, tn), lambda j: (0, j)),
                  pl.BlockSpec((1, tn), lambda j: (0, j))],
        out_specs=pl.BlockSpec((rows, tn), lambda j: (0, j)),
        compiler_params=_cparams(("arbitrary",)),
        name="adaln",
    )(c, w_ada, b_ada.reshape(1, n))


def _inproj_kernel(x_ref, sc_ref, sh_ref, g_ref, wm_ref, wg_ref, bg_ref,
                   fq_ref, fk_ref, fv_ref, mq_ref, mk_ref, mv_ref, mo_ref, gt_ref, gtt_ref):
    x = x_ref[...]
    y = x * lax.rsqrt(jnp.mean(x * x, axis=-1, keepdims=True) + NORM_EPS) * g_ref[...]
    h = y * (1.0 + sc_ref[...]) + sh_ref[...]
    hb = h.astype(BF16)
    for i, o_ref in enumerate((fq_ref, fk_ref, fv_ref, mq_ref, mk_ref, mv_ref, mo_ref)):
        o_ref[...] = jnp.dot(hb, wm_ref[:, i * 512:(i + 1) * 512], preferred_element_type=F32)
    zg = jnp.dot(h, wg_ref[...], precision=HIGHEST, preferred_element_type=F32) + bg_ref[...]
    col = lax.broadcasted_iota(jnp.int32, zg.shape, 1)
    is_i = (col >= FOX_HEADS) & (col < FOX_HEADS + MLSTM_HEADS)
    gt = jnp.where(is_i, zg, _log_sigmoid(zg))
    gt_ref[...] = gt
    gtt_ref[...] = gt.T


def in_projection(x, scale, shift, g, wm, wg, bg, tm):
    n, d = x.shape
    mod_rows = scale.shape[0]
    if mod_rows == 1:
        mod_spec = pl.BlockSpec((1, d), lambda i: (0, 0))
    else:
        mod_spec = pl.BlockSpec((tm, d), lambda i: (i, 0))
    o512 = jax.ShapeDtypeStruct((n, 512), F32)
    s512 = pl.BlockSpec((tm, 512), lambda i: (i, 0))
    return pl.pallas_call(
        _inproj_kernel,
        out_shape=[o512] * 7 + [jax.ShapeDtypeStruct((n, GATE_LANES), F32),
                                jax.ShapeDtypeStruct((GATE_LANES, n), F32)],
        grid=(n // tm,),
        in_specs=[pl.BlockSpec((tm, d), lambda i: (i, 0)), mod_spec, mod_spec,
                  pl.BlockSpec((1, d), lambda i: (0, 0)),
                  pl.BlockSpec(wm.shape, lambda i: (0, 0)),
                  pl.BlockSpec(wg.shape, lambda i: (0, 0)),
                  pl.BlockSpec((1, GATE_LANES), lambda i: (0, 0))],
        out_specs=[s512] * 7 + [pl.BlockSpec((tm, GATE_LANES), lambda i: (i, 0)),
                                pl.BlockSpec((GATE_LANES, tm), lambda i: (0, i))],
        compiler_params=_cparams(("arbitrary",)),
        name="inproj",
    )(x, scale, shift, g, wm, wg, bg)


def _lane_cumsum(x, stride=1):
    n = x.shape[-1]
    lane = lax.broadcasted_iota(jnp.int32, x.shape, x.ndim - 1)
    sh = stride
    while sh < n:
        x = x + jnp.where(lane >= sh, pltpu.roll(x, sh, x.ndim - 1), 0.0)
        sh *= 2
    return x


def _cumsum_kernel(x_ref, f_ref, hi_ref, mid_ref, lo_ref, *, blk):
    f = _lane_cumsum(x_ref[...]) * LOG2E
    f_ref[...] = f
    for b in range(f.shape[-1] // blk):
        cols = slice(b * blk, (b + 1) * blk)
        rel = f[:, cols] - f[:, b * blk:b * blk + 1]
        hi = rel.astype(BF16).astype(F32)
        mid = (rel - hi).astype(BF16).astype(F32)
        hi_ref[:, cols] = hi
        mid_ref[:, cols] = mid
        lo_ref[:, cols] = (rel - hi - mid).astype(BF16).astype(F32)


def forget_cumsum(x, blk):
    out = jax.ShapeDtypeStruct(x.shape, F32)
    return pl.pallas_call(
        functools.partial(_cumsum_kernel, blk=blk),
        out_shape=[out, out, out, out],
        name="cumsum",
    )(x)


def _rows_cumsum_kernel(x_ref, o_ref):
    o_ref[...] = _lane_cumsum(x_ref[...])


def rows_cumsum(x, tr):
    rows, n = x.shape
    return pl.pallas_call(
        _rows_cumsum_kernel,
        out_shape=jax.ShapeDtypeStruct(x.shape, F32),
        grid=(rows // tr,),
        in_specs=[pl.BlockSpec((tr, n), lambda i: (i, 0))],
        out_specs=pl.BlockSpec((tr, n), lambda i: (i, 0)),
        compiler_params=_cparams(("arbitrary",)),
        name="page_cumsum",
    )(x)


def _flash_kernel(qi_ref, ki_ref, fs_ref, qa_ref, ka_ref, vt_ref, o_ref, m_ref, l_ref, acc_ref, *, nb):
    s_idx = pl.program_id(0)
    qi = qi_ref[s_idx]
    ki = ki_ref[s_idx]
    nh = qa_ref.shape[0]

    @pl.when(ki == 0)
    def _():
        m_ref[...] = jnp.full(m_ref.shape, -jnp.inf, F32)
        l_ref[...] = jnp.zeros(l_ref.shape, F32)
        acc_ref[...] = jnp.zeros(acc_ref.shape, F32)

    def scores(h):
        return lax.dot_general(ka_ref[h], qa_ref[h], (((1,), (1,)), ((), ())),
                               preferred_element_type=F32)

    def softmax_part(h, st, masked):
        if masked:
            key = lax.broadcasted_iota(jnp.int32, st.shape, 0)
            qry = lax.broadcasted_iota(jnp.int32, st.shape, 1)
            st = jnp.where(key <= qry, st, -jnp.inf)
        cb = -fs_ref[h * nb + ki]
        m_prev = m_ref[h]
        m_new = jnp.maximum(m_prev, jnp.max(st, axis=0, keepdims=True) + cb)
        alpha = jnp.exp2(m_prev - m_new)
        p = jnp.exp2(st + (cb - m_new))
        l_ref[h] = alpha * l_ref[h] + jnp.sum(p, axis=0, keepdims=True)
        m_ref[h] = m_new
        return p.astype(BF16), alpha

    def accumulate(h, p, alpha):
        acc_ref[h] = alpha * acc_ref[h] + jnp.dot(vt_ref[h], p, preferred_element_type=F32)

    def step(masked):
        st_next = scores(0)
        pending = None
        for h in range(nh):
            st = st_next
            if h + 1 < nh:
                st_next = scores(h + 1)
            p, alpha = softmax_part(h, st, masked)
            if pending is not None:
                accumulate(*pending)
            pending = (h, p, alpha)
        accumulate(*pending)

    @pl.when(ki < qi)
    def _():
        step(False)

    @pl.when(ki == qi)
    def _():
        step(True)
        o_ref[...] = acc_ref[...] / l_ref[...]


def fox_prompt_attention(qa, ka, vt, f_start, t_blk):
    h, t, da = qa.shape
    dh = vt.shape[1]
    nb = t // t_blk
    pairs = [(i, j) for i in range(nb) for j in range(i + 1)]
    qi_arr = jnp.array([p[0] for p in pairs], jnp.int32)
    ki_arr = jnp.array([p[1] for p in pairs], jnp.int32)
    grid_spec = pltpu.PrefetchScalarGridSpec(
        num_scalar_prefetch=3,
        grid=(len(pairs),),
        in_specs=[pl.BlockSpec((h, t_blk, da), lambda s, qi, ki, fs: (0, qi[s], 0)),
                  pl.BlockSpec((h, t_blk, da), lambda s, qi, ki, fs: (0, ki[s], 0)),
                  pl.BlockSpec((h, dh, t_blk), lambda s, qi, ki, fs: (0, 0, ki[s]))],
        out_specs=pl.BlockSpec((h, dh, t_blk), lambda s, qi, ki, fs: (0, 0, qi[s])),
        scratch_shapes=[pltpu.VMEM((h, 1, t_blk), F32), pltpu.VMEM((h, 1, t_blk), F32),
                        pltpu.VMEM((h, dh, t_blk), F32)],
    )
    return pl.pallas_call(
        functools.partial(_flash_kernel, nb=nb),
        out_shape=jax.ShapeDtypeStruct((h, dh, t), F32),
        grid_spec=grid_spec,
        compiler_params=_cparams(("arbitrary",)),
        name="fox_prompt",
    )(qi_arr, ki_arr, f_start, qa, ka, vt)


def _mlstm_kernel(q_ref, k_ref, v_ref, g_ref, gt_ref, c0_ref, n0_ref, m0_ref,
                  h_ref, c_out_ref, n_out_ref, m_out_ref, c_s, n_s, m_s, *, rows, seg_len):
    nseg = rows // seg_len
    dk = MLSTM_HEAD_DIM
    c_idx = pl.program_id(1)

    @pl.when(c_idx == 0)
    def _():
        c_s[...] = c0_ref[...]
        n_s[...] = n0_ref[...]
        m_s[...] = m0_ref[...]

    g = g_ref[0]
    gt = gt_ref[0]
    row = lax.broadcasted_iota(jnp.int32, (rows, rows), 0)
    col = lax.broadcasted_iota(jnp.int32, (rows, rows), 1)
    if nseg == 1:
        causal = col <= row
        anti = row <= col
    else:
        same = (row // seg_len) == (col // seg_len)
        causal = (col <= row) & same
        anti = (row <= col) & same
    rid = lax.broadcasted_iota(jnp.int32, (rows, 1), 0) // seg_len

    def per_row(vals):
        out = jnp.broadcast_to(vals[0], (rows, 1))
        for j in range(1, nseg):
            out = jnp.where(rid == j, vals[j], out)
        return out

    for h in range(MLSTM_HEADS):
        lane = slice(h * dk, (h + 1) * dk)
        q = q_ref[0, :, lane]
        k = k_ref[0, :, lane] * (dk ** -0.5)
        v = v_ref[0, :, lane]
        i_row = gt[8 + h:9 + h, :]
        lf_row = gt[12 + h:13 + h, :]
        i_col = g[:, 8 + h:9 + h]
        lf_col = g[:, 12 + h:13 + h]
        b_col = jnp.sum(jnp.where(causal, lf_row, 0.0), axis=1, keepdims=True)
        b_row = jnp.sum(jnp.where(anti, lf_col, 0.0), axis=0, keepdims=True)
        log_d = jnp.where(causal, b_col - b_row + i_row, -jnp.inf)
        m_prev = [m_s[j, h] for j in range(nseg)]
        m_inter = b_col + per_row(m_prev)
        m_t = jnp.maximum(m_inter, jnp.max(log_d, axis=1, keepdims=True))
        d_mat = jnp.exp(log_d - m_t)
        inter = jnp.exp(m_inter - m_t)
        qb = q.astype(BF16)
        s_mat = lax.dot_general(qb, k.astype(BF16), (((1,), (1,)), ((), ())),
                                preferred_element_type=F32) * d_mat
        num = jnp.dot(s_mat.astype(BF16), v.astype(BF16), preferred_element_type=F32)
        qc = jnp.concatenate(
            [jnp.dot(qb[j * seg_len:(j + 1) * seg_len], c_s[j, h].astype(BF16),
                     preferred_element_type=F32) for j in range(nseg)], axis=0)
        n_rows = jnp.concatenate(
            [jnp.broadcast_to(n_s[j, h], (seg_len, dk)) for j in range(nseg)], axis=0)
        qn = jnp.sum(q * n_rows, axis=1, keepdims=True)
        num = inter * qc + num
        den = inter * qn + jnp.sum(s_mat, axis=1, keepdims=True)
        h_ref[0, :, lane] = num / jnp.maximum(jnp.abs(den), jnp.exp(-m_t))

        last = [(j + 1) * seg_len - 1 for j in range(nseg)]
        b_last = [b_col[r:r + 1, :] for r in last]
        m_new = [m_t[r:r + 1, :] for r in last]
        w_end = jnp.exp(per_row(b_last) - b_col + i_col - per_row(m_new))
        kw = k * w_end
        kwt = kw.T.astype(BF16)
        vb = v.astype(BF16)
        if nseg > 1:
            vb = jnp.concatenate([jnp.where(rid == j, vb, jnp.zeros_like(vb))
                                  for j in range(nseg)], axis=1)
        upd = jnp.dot(kwt, vb, preferred_element_type=F32)
        for j in range(nseg):
            decay = jnp.exp(b_last[j] + m_prev[j] - m_new[j])
            kw_j = kw if nseg == 1 else jnp.where(rid == j, kw, 0.0)
            c_s[j, h] = decay * c_s[j, h] + upd[:, j * dk:(j + 1) * dk]
            n_s[j, h] = decay * n_s[j, h] + jnp.sum(kw_j, axis=0, keepdims=True)
            m_s[j, h] = m_new[j]

    @pl.when(c_idx == pl.num_programs(1) - 1)
    def _():
        c_out_ref[...] = c_s[...]
        n_out_ref[...] = n_s[...]
        m_out_ref[...] = m_s[...]


def mlstm(q, k, v, gates, gates_t, c0, n0, m0, rows, seg_len):
    G, tg, w = q.shape
    nseg = rows // seg_len
    nc = tg // rows
    assert nseg == 1 or nc == 1
    qspec = pl.BlockSpec((1, rows, w), lambda gi, c: (gi, c, 0))
    cspec = pl.BlockSpec((nseg, MLSTM_HEADS, 128, 128), lambda gi, c: (gi, 0, 0, 0))
    nspec = pl.BlockSpec((nseg, MLSTM_HEADS, 1, 128), lambda gi, c: (gi, 0, 0, 0))
    mspec = pl.BlockSpec((nseg, MLSTM_HEADS, 1, 1), lambda gi, c: (gi, 0, 0, 0))
    return pl.pallas_call(
        functools.partial(_mlstm_kernel, rows=rows, seg_len=seg_len),
        out_shape=[jax.ShapeDtypeStruct((G, tg, w), F32),
                   jax.ShapeDtypeStruct(c0.shape, F32),
                   jax.ShapeDtypeStruct(n0.shape, F32),
                   jax.ShapeDtypeStruct(m0.shape, F32)],
        grid=(G, nc),
        in_specs=[qspec, qspec, qspec,
                  pl.BlockSpec((1, rows, GATE_LANES), lambda gi, c: (gi, c, 0)),
                  pl.BlockSpec((1, 16, rows), lambda gi, c: (gi, 0, c)),
                  cspec, nspec, mspec],
        out_specs=[qspec, cspec, nspec, mspec],
        scratch_shapes=[pltpu.VMEM((nseg, MLSTM_HEADS, 128, 128), F32),
                        pltpu.VMEM((nseg, MLSTM_HEADS, 1, 128), F32),
                        pltpu.VMEM((nseg, MLSTM_HEADS, 1, 1), F32)],
        compiler_params=_cparams(("arbitrary", "arbitrary")),
        name="mlstm",
    )(q, k, v, gates, gates_t, c0, n0, m0)


def _fox_sample_kernel(pt_ref, qbd_ref, knew_ref, vnew_ref, lfnew_ref, *rest, pps):
    k_refs = rest[0:pps]
    v_refs = rest[pps:2 * pps]
    lf_refs = rest[2 * pps:3 * pps]
    o_ref = rest[3 * pps]
    m_ref, l_ref, acc_ref, f_ref = rest[3 * pps + 1:]
    step = pl.program_id(1)
    nq, w = qbd_ref.shape[1], qbd_ref.shape[2]
    reps = nq // FOX_HEADS

    @pl.when(step == 0)
    def _():
        m_ref[...] = jnp.full(m_ref.shape, -jnp.inf, F32)
        l_ref[...] = jnp.zeros(l_ref.shape, F32)
        acc_ref[...] = jnp.zeros(acc_ref.shape, F32)
        f_ref[...] = jnp.zeros(f_ref.shape, F32)

    qbd = qbd_ref[0]

    def update(logits, pv_fn):
        m_prev = m_ref[...]
        m_new = jnp.maximum(m_prev, jnp.max(logits, axis=-1, keepdims=True))
        alpha = jnp.exp(m_prev - m_new)
        p = jnp.exp(logits - m_new)
        l_ref[...] = alpha * l_ref[...] + jnp.sum(p, axis=-1, keepdims=True)
        acc_ref[...] = alpha * acc_ref[...] + pv_fn(p.astype(BF16))
        m_ref[...] = m_new

    carry = f_ref[...]
    f_pages = []
    for i in range(pps):
        f_page = lf_refs[i][0, 0] + carry
        carry = f_page[:, PAGE_SIZE - 1:PAGE_SIZE]
        f_pages.append(jnp.concatenate([f_page] * reps, axis=0))
    f_ref[...] = carry
    k_all = jnp.concatenate([r[0, 0].reshape(w, PAGE_SIZE).astype(BF16) for r in k_refs], axis=1)
    s = jnp.dot(qbd, k_all, preferred_element_type=F32)

    def pv_pages(pb):
        v_all = jnp.concatenate([r[0, 0].reshape(w, PAGE_SIZE).astype(BF16) for r in v_refs], axis=1)
        return lax.dot_general(pb, v_all, (((1,), (1,)), ((), ())), preferred_element_type=F32)

    update(s - jnp.concatenate(f_pages, axis=1), pv_pages)

    @pl.when(step == pl.num_programs(1) - 1)
    def _():
        pad = jnp.zeros((PAGE_SIZE - knew_ref.shape[1], w), F32)
        k_new = jnp.concatenate([knew_ref[0], pad], axis=0).astype(BF16)
        v_new = jnp.concatenate([vnew_ref[0], pad], axis=0).astype(BF16)
        f_new = _lane_cumsum(lfnew_ref[0]) + carry
        lg = lax.dot_general(qbd, k_new, (((1,), (1,)), ((), ())), preferred_element_type=F32)
        lg = lg - jnp.concatenate([f_new] * reps, axis=0)
        row = lax.broadcasted_iota(jnp.int32, lg.shape, 0)
        col = lax.broadcasted_iota(jnp.int32, lg.shape, 1)
        lg = jnp.where(col <= row // FOX_HEADS, lg, -jnp.inf)
        update(lg, lambda pb: jnp.dot(pb, v_new, preferred_element_type=F32))
        o = acc_ref[...] / l_ref[...]
        row = lax.broadcasted_iota(jnp.int32, o.shape, 0)
        col = lax.broadcasted_iota(jnp.int32, o.shape, 1)
        o = jnp.where(col // FOX_HEAD_DIM == row % FOX_HEADS, o, 0.0)
        o_ref[0] = jnp.concatenate(
            [jnp.sum(o[j * FOX_HEADS:(j + 1) * FOX_HEADS], axis=0, keepdims=True)
             for j in range(reps)], axis=0)


def fox_sample_attention(page_table, qbd, k_new, v_new, lf_new_t, k_cache_t, v_cache_t, lf_cache_t, pps):
    b, nq, w = qbd.shape
    n_pages = page_table.shape[1]
    pt = page_table.reshape(-1)
    s_new = nq // FOX_HEADS

    def page_map(i, nd):
        return lambda bi, s, pt_ref: (0, pt_ref[bi * n_pages + s * pps + i]) + (0,) * nd

    def bmap(bi, s, pt_ref):
        return (bi, 0, 0)

    kv_blk = (1, 1, FOX_HEADS, FOX_HEAD_DIM, PAGE_SIZE)
    in_specs = [pl.BlockSpec((1, nq, w), bmap),
                pl.BlockSpec((1,) + k_new.shape[1:], bmap),
                pl.BlockSpec((1,) + v_new.shape[1:], bmap),
                pl.BlockSpec((1,) + lf_new_t.shape[1:], bmap)]
    in_specs += [pl.BlockSpec(kv_blk, page_map(i, 3)) for i in range(pps)]
    in_specs += [pl.BlockSpec(kv_blk, page_map(i, 3)) for i in range(pps)]
    in_specs += [pl.BlockSpec((1, 1, FOX_HEADS, PAGE_SIZE), page_map(i, 2)) for i in range(pps)]
    grid_spec = pltpu.PrefetchScalarGridSpec(
        num_scalar_prefetch=1,
        grid=(b, n_pages // pps),
        in_specs=in_specs,
        out_specs=pl.BlockSpec((1, s_new, w), bmap),
        scratch_shapes=[pltpu.VMEM((nq, 1), F32), pltpu.VMEM((nq, 1), F32),
                        pltpu.VMEM((nq, w), F32), pltpu.VMEM((FOX_HEADS, 1), F32)],
    )
    return pl.pallas_call(
        functools.partial(_fox_sample_kernel, pps=pps),
        out_shape=jax.ShapeDtypeStruct((b, s_new, w), F32),
        grid_spec=grid_spec,
        compiler_params=_cparams(("arbitrary", "arbitrary")),
        name="fox_sample",
    )(pt, qbd, k_new, v_new, lf_new_t, *([k_cache_t] * pps), *([v_cache_t] * pps),
      *([lf_cache_t] * pps))


def _group_mean_sq(x, bd):
    sq = x * x
    hi = sq.astype(BF16)
    lo = (sq - hi.astype(F32)).astype(BF16)
    return (jnp.dot(hi, bd, preferred_element_type=F32) + jnp.dot(lo, bd, preferred_element_type=F32))


def _combine_kernel(fo_ref, mh_ref, mo_ref, x_ref, g1_ref, sc2_ref, sh2_ref, gfo_ref, gml_ref,
                    gffn_ref, wout_ref, wpq_ref, keys_ref, bd64_ref, bd128_ref,
                    x1_ref, h2_ref, st_ref):
    fo = fo_ref[...]
    a_out = fo * lax.rsqrt(_group_mean_sq(fo, bd64_ref[...]) + NORM_EPS) * gfo_ref[...]
    mh = mh_ref[...]
    mo = mo_ref[...]
    b_out = (mh * lax.rsqrt(_group_mean_sq(mh, bd128_ref[...]) + NORM_EPS) * gml_ref[...]
             / (1.0 + jnp.exp(-mo)))
    cat = jnp.concatenate([a_out, b_out], axis=1).astype(BF16)
    x1 = x_ref[...] + g1_ref[...] * jnp.dot(cat, wout_ref[...], preferred_element_type=F32)
    x1_ref[...] = x1
    y = x1 * lax.rsqrt(jnp.mean(x1 * x1, axis=-1, keepdims=True) + NORM_EPS) * gffn_ref[...]
    h2 = (y * (1.0 + sc2_ref[...]) + sh2_ref[...]).astype(BF16)
    h2_ref[...] = h2
    pq = jnp.dot(h2, wpq_ref[...], preferred_element_type=F32).astype(BF16)
    for hp in range(2 * PEER_HEADS):
        st_ref[hp] = lax.dot_general(keys_ref[hp], pq[:, hp * PEER_HALF:(hp + 1) * PEER_HALF],
                                     (((1,), (1,)), ((), ())), preferred_element_type=F32)


def combine(fox_o, m_h, mo, x, gate1, scale2, shift2, g_fox, g_ml, g_ffn, w_out, w_pq, keys,
            bd64, bd128, tm):
    n, d = x.shape
    mod_rows = gate1.shape[0]
    if mod_rows == 1:
        mod_spec = pl.BlockSpec((1, d), lambda i: (0, 0))
    else:
        mod_spec = pl.BlockSpec((tm, d), lambda i: (i, 0))

    def full(a):
        return pl.BlockSpec(a.shape, lambda i: (0,) * a.ndim)

    s512 = pl.BlockSpec((tm, 512), lambda i: (i, 0))
    sd = pl.BlockSpec((tm, d), lambda i: (i, 0))
    return pl.pallas_call(
        _combine_kernel,
        out_shape=[jax.ShapeDtypeStruct((n, d), F32), jax.ShapeDtypeStruct((n, d), BF16),
                   jax.ShapeDtypeStruct((2 * PEER_HEADS, PEER_N_KEYS, n), F32)],
        grid=(n // tm,),
        in_specs=[s512, s512, s512, sd, mod_spec, mod_spec, mod_spec, full(g_fox), full(g_ml),
                  full(g_ffn), full(w_out), full(w_pq), full(keys), full(bd64), full(bd128)],
        out_specs=[sd, sd, pl.BlockSpec((2 * PEER_HEADS, PEER_N_KEYS, tm), lambda i: (0, 0, i))],
        compiler_params=_cparams(("arbitrary",)),
        name="combine",
    )(fox_o, m_h, mo, x, gate1, scale2, shift2, g_fox, g_ml, g_ffn, w_out, w_pq, keys, bd64, bd128)


def _sort_desc(a):
    a = list(a)
    n = len(a)
    k = 2
    while k <= n:
        j = k // 2
        while j >= 1:
            for i in range(n):
                p = i ^ j
                if p > i:
                    hi, lo = jnp.maximum(a[i], a[p]), jnp.minimum(a[i], a[p])
                    a[i], a[p] = (hi, lo) if (i & k) == 0 else (lo, hi)
            j //= 2
        k *= 2
    return a


def _merge_desc(a):
    a = list(a)
    j = len(a) // 2
    while j >= 1:
        for i in range(len(a)):
            p = i ^ j
            if p > i:
                a[i], a[p] = jnp.maximum(a[i], a[p]), jnp.minimum(a[i], a[p])
        j //= 2
    return a


_PEER_CAND = [(i, j) for i in range(PEER_TOPK) for j in range(PEER_TOPK)
              if (i + 1) * (j + 1) <= PEER_TOPK]


def _top16_of_128(slabs):
    t = _sort_desc(slabs)
    sub = lax.broadcasted_iota(jnp.int32, t[0].shape, 0)
    for bit in (1, 2, 4):
        if bit == 4:
            partner = [pltpu.roll(v, 4, 0) for v in t]
        else:
            low = (sub & bit) == 0
            partner = [jnp.where(low, pltpu.roll(v, 8 - bit, 0), pltpu.roll(v, bit, 0)) for v in t]
        t = _merge_desc([jnp.maximum(t[i], partner[PEER_TOPK - 1 - i]) for i in range(PEER_TOPK)])
    return t


def _kth_candidate_sums(t1, t2):
    c = [t1[i] + t2[j] for (i, j) in _PEER_CAND]
    c += [jnp.full(c[0].shape, NEG_BIG, F32)] * (64 - len(c))
    return _sort_desc(c)


def _topk_kernel(st_ref, u1_ref, u2_ref, th_ref, t1_s, t2_s):
    nslab = PEER_N_KEYS // 8

    def pack(t, t_s, h):
        for i in range(PEER_TOPK):
            t_s[i, h:h + 1, :] = t[i][0:1, :]

    for h in range(PEER_HEADS):
        u1_ref[h] = st_ref[2 * h] * LOG2E
        u2_ref[h] = st_ref[2 * h + 1] * LOG2E
        pack(_top16_of_128([u1_ref[h, 8 * i:8 * i + 8, :] for i in range(nslab)]), t1_s, h)
        pack(_top16_of_128([u2_ref[h, 8 * i:8 * i + 8, :] for i in range(nslab)]), t2_s, h)
    t1 = [t1_s[i] for i in range(PEER_TOPK)]
    c = _kth_candidate_sums(t1, [t2_s[i] for i in range(PEER_TOPK)])
    z = jnp.exp2(c[0] - c[0])
    for r in range(1, PEER_TOPK):
        z = z + jnp.exp2(c[r] - c[0])
    k2 = c[0] + jnp.log2(z)
    for h in range(PEER_HEADS):
        u2_ref[h] = u2_ref[h] - k2[h:h + 1, :]
        pack(_top16_of_128([u2_ref[h, 8 * i:8 * i + 8, :] for i in range(nslab)]), t2_s, h)
    c = _kth_candidate_sums(t1, [t2_s[i] for i in range(PEER_TOPK)])
    th_ref[...] = c[PEER_TOPK - 1]


def peer_topk(st):
    tm = 128
    hp, nk, n = st.shape
    uspec = pl.BlockSpec((PEER_HEADS, nk, tm), lambda i: (0, 0, i))
    ushape = jax.ShapeDtypeStruct((PEER_HEADS, nk, n), F32)
    return pl.pallas_call(
        _topk_kernel,
        out_shape=[ushape, ushape, jax.ShapeDtypeStruct((PEER_HEADS, n), F32)],
        grid=(n // tm,),
        in_specs=[pl.BlockSpec((hp, nk, tm), lambda i: (0, 0, i))],
        out_specs=[uspec, uspec, pl.BlockSpec((PEER_HEADS, tm), lambda i: (0, i))],
        scratch_shapes=[pltpu.VMEM((PEER_TOPK, 8, 128), F32), pltpu.VMEM((PEER_TOPK, 8, 128), F32)],
        compiler_params=_cparams(("arbitrary",)),
        name="peer_topk",
    )(st)


def _gelu(x):
    return 0.5 * x * (1.0 + lax.erf(x * (2.0 ** -0.5)))


def _peer_kernel(u1_ref, u2_ref, th_ref, h2_ref, u_ref, vt_ref, x1_ref, g2_ref, gf_ref, o_ref,
                 acc_ref, at_ref, p_ref, *, tm, te):
    e = pl.program_id(1)

    @pl.when(e == 0)
    def _():
        acc_ref[...] = jnp.zeros(acc_ref.shape, F32)

    at_ref[...] = lax.dot_general(u_ref[...], h2_ref[...], (((1,), (1,)), ((), ())),
                                  preferred_element_type=F32)
    a_per_blk = te // PEER_N_KEYS

    heads = range(PEER_HEADS)

    def weighted(r8, th8, rows, lanes):
        w = None
        for h in heads:
            c = r8[h] + u2_ref[h, rows.start % PEER_N_KEYS:rows.start % PEER_N_KEYS + 8, lanes]
            e = jnp.where(c >= th8[h], jnp.exp2(c), 0.0)
            w = e if w is None else w + e
        return w * _gelu(at_ref[rows, lanes])

    for tg in range(tm // 128):
        lanes = slice(tg * 128, (tg + 1) * 128)
        th8 = [jnp.broadcast_to(th_ref[h:h + 1, lanes], (8, 128)) for h in heads]
        for al in range(a_per_blk):
            r8 = [jnp.broadcast_to(u1_ref[h, al:al + 1, lanes], (8, 128)) for h in heads]
            for i in range(0, PEER_N_KEYS, 16):
                r0 = al * PEER_N_KEYS + i
                pair = [weighted(r8, th8, slice(r0 + d, r0 + d + 8), lanes) for d in (0, 8)]
                p_ref[r0:r0 + 16, lanes] = jnp.concatenate(pair, axis=0).astype(BF16)
    acc_ref[...] += jnp.dot(vt_ref[...], p_ref[...], preferred_element_type=F32)

    @pl.when(e == pl.num_programs(1) - 1)
    def _():
        xo = x1_ref[...] + g2_ref[...] * acc_ref[...].T
        o_ref[...] = xo * lax.rsqrt(jnp.mean(xo * xo, axis=-1, keepdims=True) + NORM_EPS) * gf_ref[...]


def peer_experts(u1, u2, th, h2, u_bf, vt_bf, x1, gate2, g_final, tm, te):
    n, d = x1.shape
    n_exp = u_bf.shape[0]
    mod_rows = gate2.shape[0]
    if mod_rows == 1:
        mod_spec = pl.BlockSpec((1, d), lambda i, e: (0, 0))
    else:
        mod_spec = pl.BlockSpec((tm, d), lambda i, e: (i, 0))
    uspec = pl.BlockSpec((PEER_HEADS, PEER_N_KEYS, tm), lambda i, e: (0, 0, i))
    u1spec = pl.BlockSpec((PEER_HEADS, te // PEER_N_KEYS, tm), lambda i, e: (0, e, i))
    return pl.pallas_call(
        functools.partial(_peer_kernel, tm=tm, te=te),
        out_shape=jax.ShapeDtypeStruct((n, d), F32),
        grid=(n // tm, n_exp // te),
        in_specs=[u1spec, uspec, pl.BlockSpec((PEER_HEADS, tm), lambda i, e: (0, i)),
                  pl.BlockSpec((tm, d), lambda i, e: (i, 0)),
                  pl.BlockSpec((te, d), lambda i, e: (e, 0)),
                  pl.BlockSpec((d, te), lambda i, e: (0, e)),
                  pl.BlockSpec((tm, d), lambda i, e: (i, 0)), mod_spec,
                  pl.BlockSpec((1, d), lambda i, e: (0, 0))],
        out_specs=pl.BlockSpec((tm, d), lambda i, e: (i, 0)),
        scratch_shapes=[pltpu.VMEM((d, tm), F32), pltpu.VMEM((te, tm), F32),
                        pltpu.VMEM((te, tm), BF16)],
        compiler_params=_cparams(("arbitrary", "arbitrary")),
        name="peer_experts",
    )(u1, u2, th, h2, u_bf, vt_bf, x1, gate2, g_final)


def _fox_prompt(fq, fk, fv, logf_t, t_blk):
    t = fq.shape[0]
    nh, dh = FOX_HEADS, FOX_HEAD_DIM
    f2, hi, mid, lo = forget_cumsum(logf_t, t_blk)

    def heads(a):
        return a.astype(BF16).reshape(t, nh, dh).transpose(1, 0, 2)

    aug = 128 - dh - 3
    qa = jnp.concatenate([heads(fq * (dh ** -0.5 * LOG2E)), jnp.full((nh, t, 3), -1.0, BF16),
                          jnp.zeros((nh, t, aug), BF16)], axis=-1)
    ka = jnp.concatenate([heads(fk), jnp.stack([hi, mid, lo], axis=-1).astype(BF16),
                          jnp.zeros((nh, t, aug), BF16)], axis=-1)
    vt = fv.astype(BF16).reshape(t, nh, dh).transpose(1, 2, 0)
    o_t = fox_prompt_attention(qa, ka, vt, f2[:, ::t_blk].reshape(-1), t_blk)
    return o_t.reshape(nh * dh, t).T


def _prep_weights(w_in, b_fox_f, b_mlstm_i, b_mlstm_f):
    fw, mw = FOX_WIDTH, MLSTM_WIDTH
    o = 3 * fw + FOX_HEADS
    g0 = o + 3 * mw
    wm = jnp.concatenate([w_in[:, :3 * fw], w_in[:, o:o + 3 * mw], w_in[:, g0 + 2 * MLSTM_HEADS:]],
                         axis=1).astype(BF16)
    wg = jnp.concatenate([w_in[:, 3 * fw:o], w_in[:, g0:g0 + 2 * MLSTM_HEADS]], axis=1)
    wg = jnp.pad(wg, ((0, 0), (0, GATE_LANES - wg.shape[1])))
    bg = jnp.concatenate([b_fox_f, b_mlstm_i, b_mlstm_f])
    bg = jnp.pad(bg, (0, GATE_LANES - bg.shape[0])).reshape(1, GATE_LANES)
    return wm, wg, bg


def kernel(x_prompt, x_sample, c_prompt, c_sample, cache_fox_k, cache_fox_v, cache_fox_logf, state_mlstm_C, state_mlstm_n, state_mlstm_m, page_table, w_ada, b_ada, g_norm_mix, g_norm_ffn, w_in, b_fox_f, b_mlstm_i, b_mlstm_f, g_fox_out, g_mlstm_out, w_out, w_peer_q, peer_keys, peer_u, peer_v, g_final):
    D = D_MODEL
    T = x_prompt.shape[1]
    B, S = x_sample.shape[0], x_sample.shape[1]
    n_pages = page_table.shape[1]
    H, DH = FOX_HEADS, FOX_HEAD_DIM
    MH, DK = MLSTM_HEADS, MLSTM_HEAD_DIM
    SP = 8
    SEG_PER_BLK = 128 // SP

    wm, wg, bg = _prep_weights(w_in[0], b_fox_f[0], b_mlstm_i[0], b_mlstm_f[0])
    w_out_b = w_out[0].astype(BF16)
    w_pq_b = w_peer_q[0].astype(BF16)
    keys_b = peer_keys[0].reshape(2 * PEER_HEADS, PEER_N_KEYS, PEER_HALF).astype(BF16)
    u_b = peer_u[0].astype(BF16)
    vt_b = peer_v[0].T.astype(BF16)
    lane = jnp.arange(FOX_WIDTH)
    bd64 = jnp.where(lane[:, None] // DH == lane[None, :] // DH, 1.0 / DH, 0.0).astype(BF16)
    bd128 = jnp.where(lane[:, None] // DK == lane[None, :] // DK, 1.0 / DK, 0.0).astype(BF16)
    g_mix = g_norm_mix[0].reshape(1, D)
    g_ffn = g_norm_ffn[0].reshape(1, D)
    g_fox = g_fox_out[0].reshape(1, FOX_WIDTH)
    g_ml = g_mlstm_out[0].reshape(1, MLSTM_WIDTH)
    g_fin = g_final.reshape(1, D)

    c_all = jnp.concatenate([c_prompt, c_sample], axis=0)
    c_all = jnp.pad(c_all, ((0, (-c_all.shape[0]) % 8), (0, 0)))
    ada = adaln_terms(c_all, w_ada[0], b_ada[0])

    def terms(a):
        return [a[:, i * D:(i + 1) * D] for i in range(6)]

    shift1_p, scale1_p, gate1_p, shift2_p, scale2_p, gate2_p = terms(ada[0:1])
    shift1_s, scale1_s, gate1_s, shift2_s, scale2_s, gate2_s = terms(
        jnp.repeat(ada[1:1 + B], S, axis=0))

    xp = x_prompt.reshape(T, D)
    fq, fk, fv, mq, mk, mv, mo, gt, gtt = in_projection(xp, scale1_p, shift1_p, g_mix, wm, wg, bg, 256)
    fox_o = _fox_prompt(fq, fk, fv, gtt[0:H], 512)
    m_h, c_p, n_p, m_p = mlstm(
        mq.reshape(1, T, MLSTM_WIDTH), mk.reshape(1, T, MLSTM_WIDTH), mv.reshape(1, T, MLSTM_WIDTH),
        gt.reshape(1, T, GATE_LANES), gtt[0:16].reshape(1, 16, T),
        jnp.zeros((1, MH, DK, DK), F32), jnp.zeros((1, MH, 1, DK), F32), jnp.zeros((1, MH, 1, 1), F32),
        256, 256)
    x1, h2, st = combine(fox_o, m_h.reshape(T, MLSTM_WIDTH), mo, xp, gate1_p, scale2_p, shift2_p,
                         g_fox, g_ml, g_ffn, w_out_b, w_pq_b, keys_b, bd64, bd128, 256)
    u1, u2, th = peer_topk(st)
    y_p = peer_experts(u1, u2, th, h2, u_b, vt_b, x1, gate2_p, g_fin, 512, 1024)

    ns = B * S
    xs = x_sample.reshape(ns, D)
    sfq, sfk, sfv, smq, smk, smv, smo, sgt, _ = in_projection(
        xs, scale1_s, shift1_s, g_mix, wm, wg, bg, 256)
    def pad_tok(a):
        a = a.reshape(B, S, a.shape[-1])
        return jnp.pad(a, ((0, 0), (0, SP - S), (0, 0)))

    q4 = (sfq * DH ** -0.5).reshape(B, S, H, 1, DH)
    qbd = (q4 * jnp.eye(H, dtype=F32)[None, None, :, :, None]).astype(BF16).reshape(B, S * H, FOX_WIDTH)
    lf_new_t = jnp.pad(sgt[:, 0:H].reshape(B, S, H).transpose(0, 2, 1),
                       ((0, 0), (0, 0), (0, PAGE_SIZE - S)))
    n_pool = cache_fox_k.shape[1]
    lf_pages = rows_cumsum(cache_fox_logf.transpose(0, 1, 3, 2).reshape(n_pool * H, PAGE_SIZE), 8192)
    so = fox_sample_attention(
        page_table, qbd, pad_tok(sfk), pad_tok(sfv), lf_new_t,
        cache_fox_k.transpose(0, 1, 3, 4, 2), cache_fox_v.transpose(0, 1, 3, 4, 2),
        lf_pages.reshape(1, n_pool, H, PAGE_SIZE), 8)
    pad_gate = jnp.zeros((GATE_LANES,), F32).at[H:H + MH].set(NEG_BIG)
    sg = jnp.concatenate([sgt.reshape(B, S, GATE_LANES),
                          jnp.broadcast_to(pad_gate, (B, SP - S, GATE_LANES))], axis=1)
    sg = sg.reshape(B // SEG_PER_BLK, 128, GATE_LANES)

    def blk(a):
        return pad_tok(a).reshape(B // SEG_PER_BLK, 128, a.shape[-1])

    sm_h, c_s, n_s, m_s = mlstm(
        blk(smq), blk(smk), blk(smv), sg, sg.transpose(0, 2, 1)[:, 0:16],
        state_mlstm_C[0], state_mlstm_n[0].reshape(B, MH, 1, DK), state_mlstm_m[0].reshape(B, MH, 1, 1),
        128, SP)
    sm_h = sm_h.reshape(B, SP, MLSTM_WIDTH)[:, 0:S].reshape(ns, MLSTM_WIDTH)
    sx1, sh2, sst = combine(so.reshape(ns, FOX_WIDTH), sm_h, smo, xs, gate1_s, scale2_s, shift2_s,
                            g_fox, g_ml, g_ffn, w_out_b, w_pq_b, keys_b, bd64, bd128, 256)
    su1, su2, sth = peer_topk(sst)
    y_s = peer_experts(su1, su2, sth, sh2, u_b, vt_b, sx1, gate2_s, g_fin, 512, 1024)

    return (y_p.reshape(1, T, D), y_s.reshape(B, S, D),
            fk.reshape(1, 1, T, H, DH), fv.reshape(1, 1, T, H, DH), gt[:, 0:H].reshape(1, 1, T, H),
            c_p.reshape(1, 1, MH, DK, DK), n_p.reshape(1, 1, MH, DK), m_p.reshape(1, 1, MH),
            sfk.reshape(1, B, S, H, DH), sfv.reshape(1, B, S, H, DH), sgt[:, 0:H].reshape(1, B, S, H),
            c_s.reshape(1, B, MH, DK, DK), n_s.reshape(1, B, MH, DK), m_s.reshape(1, B, MH))
```

```python
import functools
import math

import jax
import jax.numpy as jnp
from jax import lax
from jax.experimental import pallas as pl
from jax.experimental.pallas import tpu as pltpu

F32 = jnp.float32
BF16 = jnp.bfloat16
HIGHEST = lax.Precision.HIGHEST

D_MODEL = 1024
FOX_HEADS = 8
FOX_HEAD_DIM = 64
FOX_WIDTH = FOX_HEADS * FOX_HEAD_DIM
MLSTM_HEADS = 4
MLSTM_HEAD_DIM = 128
MLSTM_WIDTH = MLSTM_HEADS * MLSTM_HEAD_DIM
PAGE_SIZE = 128
PEER_HEADS = 8
PEER_N_KEYS = 128
PEER_TOPK = 16
PEER_HALF = 128
NORM_EPS = 1e-6
GATE_LANES = 128
NEG_BIG = -1e30
LOG2E = 1.4426950408889634
VMEM_LIMIT = 56 * 1024 * 1024


def _cparams(sem, flags=None):
    return pltpu.CompilerParams(dimension_semantics=sem, vmem_limit_bytes=VMEM_LIMIT, flags=flags)


def _log_sigmoid(x):
    return jnp.minimum(x, 0.0) - jnp.log1p(jnp.exp(-jnp.abs(x)))


def _adaln_kernel(c_ref, w_ref, b_ref, o_ref):
    c = c_ref[...]
    s = c / (1.0 + jnp.exp(-c))
    o_ref[...] = jnp.dot(s, w_ref[...], precision=HIGHEST, preferred_element_type=F32) + b_ref[...]


def adaln_terms(c, w_ada, b_ada):
    rows, d = c.shape
    n = w_ada.shape[1]
    tn = 768
    return pl.pallas_call(
        _adaln_kernel,
        out_shape=jax.ShapeDtypeStruct((rows, n), F32),
        grid=(n // tn,),
        in_specs=[pl.BlockSpec((rows, d), lambda j: (0, 0)),
                  pl.BlockSpec((d, tn), lambda j: (0, j)),
                  pl.BlockSpec((1, tn), lambda j: (0, j))],
        out_specs=pl.BlockSpec((rows, tn), lambda j: (0, j)),
        compiler_params=_cparams(("arbitrary",)),
        name="adaln",
    )(c, w_ada, b_ada.reshape(1, n))


def _inproj_kernel(x_ref, sc_ref, sh_ref, g_ref, wm_ref, wg_ref, bg_ref,
                   fq_ref, fk_ref, fv_ref, mq_ref, mk_ref, mv_ref, mo_ref, gt_ref, gtt_ref):
    x = x_ref[...]
    y = x * lax.rsqrt(jnp.mean(x * x, axis=-1, keepdims=True) + NORM_EPS) * g_ref[...]
    h = y * (1.0 + sc_ref[...]) + sh_ref[...]
    hb = h.astype(BF16)
    for i, o_ref in enumerate((fq_ref, fk_ref, fv_ref, mq_ref, mk_ref, mv_ref, mo_ref)):
        o_ref[...] = jnp.dot(hb, wm_ref[:, i * 512:(i + 1) * 512], preferred_element_type=F32)
    zg = jnp.dot(h, wg_ref[...], precision=HIGHEST, preferred_element_type=F32) + bg_ref[...]
    col = lax.broadcasted_iota(jnp.int32, zg.shape, 1)
    is_i = (col >= FOX_HEADS) & (col < FOX_HEADS + MLSTM_HEADS)
    gt = jnp.where(is_i, zg, _log_sigmoid(zg))
    gt_ref[...] = gt
    gtt_ref[...] = gt.T


def in_projection(x, scale, shift, g, wm, wg, bg, tm):
    n, d = x.shape
    mod_rows = scale.shape[0]
    if mod_rows == 1:
        mod_spec = pl.BlockSpec((1, d), lambda i: (0, 0))
    else:
        mod_spec = pl.BlockSpec((tm, d), lambda i: (i, 0))
    o512 = jax.ShapeDtypeStruct((n, 512), F32)
    s512 = pl.BlockSpec((tm, 512), lambda i: (i, 0))
    return pl.pallas_call(
        _inproj_kernel,
        out_shape=[o512] * 7 + [jax.ShapeDtypeStruct((n, GATE_LANES), F32),
                                jax.ShapeDtypeStruct((GATE_LANES, n), F32)],
        grid=(n // tm,),
        in_specs=[pl.BlockSpec((tm, d), lambda i: (i, 0)), mod_spec, mod_spec,
                  pl.BlockSpec((1, d), lambda i: (0, 0)),
                  pl.BlockSpec(wm.shape, lambda i: (0, 0)),
                  pl.BlockSpec(wg.shape, lambda i: (0, 0)),
                  pl.BlockSpec((1, GATE_LANES), lambda i: (0, 0))],
        out_specs=[s512] * 7 + [pl.BlockSpec((tm, GATE_LANES), lambda i: (i, 0)),
                                pl.BlockSpec((GATE_LANES, tm), lambda i: (0, i))],
        compiler_params=_cparams(("arbitrary",)),
        name="inproj",
    )(x, scale, shift, g, wm, wg, bg)


def _lane_cumsum(x, stride=1):
    n = x.shape[-1]
    lane = lax.broadcasted_iota(jnp.int32, x.shape, x.ndim - 1)
    sh = stride
    while sh < n:
        x = x + jnp.where(lane >= sh, pltpu.roll(x, sh, x.ndim - 1), 0.0)
        sh *= 2
    return x


def _cumsum_kernel(x_ref, f_ref, hi_ref, mid_ref, lo_ref, *, blk):
    f = _lane_cumsum(x_ref[...]) * LOG2E
    f_ref[...] = f
    for b in range(f.shape[-1] // blk):
        cols = slice(b * blk, (b + 1) * blk)
        rel = f[:, cols] - f[:, b * blk:b * blk + 1]
        hi = rel.astype(BF16).astype(F32)
        mid = (rel - hi).astype(BF16).astype(F32)
        hi_ref[:, cols] = hi
        mid_ref[:, cols] = mid
        lo_ref[:, cols] = (rel - hi - mid).astype(BF16).astype(F32)


def forget_cumsum(x, blk):
    out = jax.ShapeDtypeStruct(x.shape, F32)
    return pl.pallas_call(
        functools.partial(_cumsum_kernel, blk=blk),
        out_shape=[out, out, out, out],
        name="cumsum",
    )(x)


def _rows_cumsum_kernel(x_ref, o_ref):
    o_ref[...] = _lane_cumsum(x_ref[...])


def rows_cumsum(x, tr):
    rows, n = x.shape
    return pl.pallas_call(
        _rows_cumsum_kernel,
        out_shape=jax.ShapeDtypeStruct(x.shape, F32),
        grid=(rows // tr,),
        in_specs=[pl.BlockSpec((tr, n), lambda i: (i, 0))],
        out_specs=pl.BlockSpec((tr, n), lambda i: (i, 0)),
        compiler_params=_cparams(("arbitrary",)),
        name="page_cumsum",
    )(x)


def _flash_kernel(qi_ref, ki_ref, fs_ref, qa_ref, ka_ref, vt_ref, o_ref, m_ref, l_ref, acc_ref, *, nb):
    s_idx = pl.program_id(0)
    qi = qi_ref[s_idx]
    ki = ki_ref[s_idx]
    nh, da, tq = qa_ref.shape
    dh = vt_ref.shape[1]

    @pl.when(ki == 0)
    def _():
        m_ref[...] = jnp.full(m_ref.shape, -jnp.inf, F32)
        l_ref[...] = jnp.zeros(l_ref.shape, F32)
        acc_ref[...] = jnp.zeros(acc_ref.shape, F32)

    def mask(x):
        key = lax.broadcasted_iota(jnp.int32, x.shape, 0)
        qry = lax.broadcasted_iota(jnp.int32, x.shape, 1)
        return jnp.where(key <= qry, x, -jnp.inf)

    def new_max(h, masked):
        raw = jnp.dot(ka_ref[h], qa_ref[h], preferred_element_type=F32)
        if masked:
            raw = mask(raw)
        cb = -fs_ref[h * nb + ki]
        m_prev = m_ref[h]
        m_new = jnp.maximum(m_prev, jnp.max(raw, axis=0, keepdims=True) + cb)
        m_ref[h] = m_new
        x = m_new - cb
        hi = x.astype(BF16).astype(F32)
        mid = (x - hi).astype(BF16).astype(F32)
        lo = (x - hi - mid).astype(BF16).astype(F32)
        row = lax.broadcasted_iota(jnp.int32, (16, tq), 0)
        aug = jnp.where(row < 3, -1.0,
                        jnp.where(row == 3, hi, jnp.where(row == 4, mid, jnp.where(row == 5, lo, 0.0))))
        q2 = jnp.concatenate([qa_ref[h, 0:dh, :], aug.astype(BF16),
                              jnp.zeros((da - dh - 16, tq), BF16)], axis=0)
        return q2, jnp.exp2(m_prev - m_new)

    def probabilities(h, q2, alpha, masked):
        raw = jnp.dot(ka_ref[h], q2, preferred_element_type=F32)
        if masked:
            raw = mask(raw)
        p = jnp.exp2(raw)
        l_ref[h] = alpha * l_ref[h] + jnp.sum(p, axis=0, keepdims=True)
        return p.astype(BF16)

    def accumulate(h, p, alpha):
        acc_ref[h] = alpha * acc_ref[h] + jnp.dot(vt_ref[h], p, preferred_element_type=F32)

    def step(masked):
        nxt = new_max(0, masked)
        pending = None
        for h in range(nh):
            q2, alpha = nxt
            if h + 1 < nh:
                nxt = new_max(h + 1, masked)
            p = probabilities(h, q2, alpha, masked)
            if pending is not None:
                accumulate(*pending)
            pending = (h, p, alpha)
        accumulate(*pending)

    @pl.when(ki < qi)
    def _():
        step(False)

    @pl.when(ki == qi)
    def _():
        step(True)
        o_ref[...] = acc_ref[...] / l_ref[...]


def fox_prompt_attention(qa, ka, vt, f_start, t_blk):
    h, t, da = ka.shape
    dh = vt.shape[1]
    nb = t // t_blk
    pairs = [(i, j) for i in range(nb) for j in range(i + 1)]
    qi_arr = jnp.array([p[0] for p in pairs], jnp.int32)
    ki_arr = jnp.array([p[1] for p in pairs], jnp.int32)
    grid_spec = pltpu.PrefetchScalarGridSpec(
        num_scalar_prefetch=3,
        grid=(len(pairs),),
        in_specs=[pl.BlockSpec((h, da, t_blk), lambda s, qi, ki, fs: (0, 0, qi[s])),
                  pl.BlockSpec((h, t_blk, da), lambda s, qi, ki, fs: (0, ki[s], 0)),
                  pl.BlockSpec((h, dh, t_blk), lambda s, qi, ki, fs: (0, 0, ki[s]))],
        out_specs=pl.BlockSpec((h, dh, t_blk), lambda s, qi, ki, fs: (0, 0, qi[s])),
        scratch_shapes=[pltpu.VMEM((h, 1, t_blk), F32), pltpu.VMEM((h, 1, t_blk), F32),
                        pltpu.VMEM((h, dh, t_blk), F32)],
    )
    return pl.pallas_call(
        functools.partial(_flash_kernel, nb=nb),
        out_shape=jax.ShapeDtypeStruct((h, dh, t), F32),
        grid_spec=grid_spec,
        compiler_params=_cparams(("arbitrary",)),
        name="fox_prompt",
    )(qi_arr, ki_arr, f_start, qa, ka, vt)


def _mlstm_kernel(q_ref, k_ref, v_ref, g_ref, gt_ref, c0_ref, n0_ref, m0_ref,
                  h_ref, c_out_ref, n_out_ref, m_out_ref, c_s, n_s, m_s, *, rows, seg_len):
    nseg = rows // seg_len
    dk = MLSTM_HEAD_DIM
    c_idx = pl.program_id(1)

    @pl.when(c_idx == 0)
    def _():
        c_s[...] = c0_ref[...]
        n_s[...] = n0_ref[...]
        m_s[...] = m0_ref[...]

    g = g_ref[0]
    gt = gt_ref[0]
    row = lax.broadcasted_iota(jnp.int32, (rows, rows), 0)
    col = lax.broadcasted_iota(jnp.int32, (rows, rows), 1)
    if nseg == 1:
        causal = col <= row
        anti = row <= col
    else:
        same = (row // seg_len) == (col // seg_len)
        causal = (col <= row) & same
        anti = (row <= col) & same
    rid = lax.broadcasted_iota(jnp.int32, (rows, 1), 0) // seg_len

    def per_row(vals):
        out = jnp.broadcast_to(vals[0], (rows, 1))
        for j in range(1, nseg):
            out = jnp.where(rid == j, vals[j], out)
        return out

    for h in range(MLSTM_HEADS):
        lane = slice(h * dk, (h + 1) * dk)
        q = q_ref[0, :, lane]
        k = k_ref[0, :, lane] * (dk ** -0.5)
        v = v_ref[0, :, lane]
        i_row = gt[8 + h:9 + h, :]
        lf_row = gt[12 + h:13 + h, :]
        i_col = g[:, 8 + h:9 + h]
        lf_col = g[:, 12 + h:13 + h]
        b_col = jnp.sum(jnp.where(causal, lf_row, 0.0), axis=1, keepdims=True)
        b_row = jnp.sum(jnp.where(anti, lf_col, 0.0), axis=0, keepdims=True)
        log_d = jnp.where(causal, b_col - b_row + i_row, -jnp.inf)
        m_prev = [m_s[j, h] for j in range(nseg)]
        m_inter = b_col + per_row(m_prev)
        m_t = jnp.maximum(m_inter, jnp.max(log_d, axis=1, keepdims=True))
        d_mat = jnp.exp(log_d - m_t)
        inter = jnp.exp(m_inter - m_t)
        qb = q.astype(BF16)
        s_mat = lax.dot_general(qb, k.astype(BF16), (((1,), (1,)), ((), ())),
                                preferred_element_type=F32) * d_mat
        num = jnp.dot(s_mat.astype(BF16), v.astype(BF16), preferred_element_type=F32)
        qc = jnp.concatenate(
            [jnp.dot(qb[j * seg_len:(j + 1) * seg_len], c_s[j, h].astype(BF16),
                     preferred_element_type=F32) for j in range(nseg)], axis=0)
        n_rows = jnp.concatenate(
            [jnp.broadcast_to(n_s[j, h], (seg_len, dk)) for j in range(nseg)], axis=0)
        qn = jnp.sum(q * n_rows, axis=1, keepdims=True)
        num = inter * qc + num
        den = inter * qn + jnp.sum(s_mat, axis=1, keepdims=True)
        h_ref[0, :, lane] = num / jnp.maximum(jnp.abs(den), jnp.exp(-m_t))

        last = [(j + 1) * seg_len - 1 for j in range(nseg)]
        b_last = [b_col[r:r + 1, :] for r in last]
        m_new = [m_t[r:r + 1, :] for r in last]
        w_end = jnp.exp(per_row(b_last) - b_col + i_col - per_row(m_new))
        kw = k * w_end
        kwt = kw.T.astype(BF16)
        vb = v.astype(BF16)
        if nseg > 1:
            vb = jnp.concatenate([jnp.where(rid == j, vb, jnp.zeros_like(vb))
                                  for j in range(nseg)], axis=1)
        upd = jnp.dot(kwt, vb, preferred_element_type=F32)
        for j in range(nseg):
            decay = jnp.exp(b_last[j] + m_prev[j] - m_new[j])
            kw_j = kw if nseg == 1 else jnp.where(rid == j, kw, 0.0)
            c_s[j, h] = decay * c_s[j, h] + upd[:, j * dk:(j + 1) * dk]
            n_s[j, h] = decay * n_s[j, h] + jnp.sum(kw_j, axis=0, keepdims=True)
            m_s[j, h] = m_new[j]

    @pl.when(c_idx == pl.num_programs(1) - 1)
    def _():
        c_out_ref[...] = c_s[...]
        n_out_ref[...] = n_s[...]
        m_out_ref[...] = m_s[...]


def mlstm(q, k, v, gates, gates_t, c0, n0, m0, rows, seg_len):
    G, tg, w = q.shape
    nseg = rows // seg_len
    nc = tg // rows
    assert nseg == 1 or nc == 1
    qspec = pl.BlockSpec((1, rows, w), lambda gi, c: (gi, c, 0))
    cspec = pl.BlockSpec((nseg, MLSTM_HEADS, 128, 128), lambda gi, c: (gi, 0, 0, 0))
    nspec = pl.BlockSpec((nseg, MLSTM_HEADS, 1, 128), lambda gi, c: (gi, 0, 0, 0))
    mspec = pl.BlockSpec((nseg, MLSTM_HEADS, 1, 1), lambda gi, c: (gi, 0, 0, 0))
    return pl.pallas_call(
        functools.partial(_mlstm_kernel, rows=rows, seg_len=seg_len),
        out_shape=[jax.ShapeDtypeStruct((G, tg, w), F32),
                   jax.ShapeDtypeStruct(c0.shape, F32),
                   jax.ShapeDtypeStruct(n0.shape, F32),
                   jax.ShapeDtypeStruct(m0.shape, F32)],
        grid=(G, nc),
        in_specs=[qspec, qspec, qspec,
                  pl.BlockSpec((1, rows, GATE_LANES), lambda gi, c: (gi, c, 0)),
                  pl.BlockSpec((1, 16, rows), lambda gi, c: (gi, 0, c)),
                  cspec, nspec, mspec],
        out_specs=[qspec, cspec, nspec, mspec],
        scratch_shapes=[pltpu.VMEM((nseg, MLSTM_HEADS, 128, 128), F32),
                        pltpu.VMEM((nseg, MLSTM_HEADS, 1, 128), F32),
                        pltpu.VMEM((nseg, MLSTM_HEADS, 1, 1), F32)],
        compiler_params=_cparams(("arbitrary", "arbitrary")),
        name="mlstm",
    )(q, k, v, gates, gates_t, c0, n0, m0)


def _fox_sample_kernel(pt_ref, qbd_ref, knew_ref, vnew_ref, lfnew_ref, *rest, pps):
    k_refs = rest[0:pps]
    v_refs = rest[pps:2 * pps]
    lf_refs = rest[2 * pps:3 * pps]
    o_ref = rest[3 * pps]
    m_ref, l_ref, acc_ref, f_ref = rest[3 * pps + 1:]
    step = pl.program_id(1)
    nq, w = qbd_ref.shape[1], qbd_ref.shape[2]
    reps = nq // FOX_HEADS

    @pl.when(step == 0)
    def _():
        m_ref[...] = jnp.full(m_ref.shape, -jnp.inf, F32)
        l_ref[...] = jnp.zeros(l_ref.shape, F32)
        acc_ref[...] = jnp.zeros(acc_ref.shape, F32)
        f_ref[...] = jnp.zeros(f_ref.shape, F32)

    qbd = qbd_ref[0]

    def update(logits, pv_fn):
        m_prev = m_ref[...]
        m_new = jnp.maximum(m_prev, jnp.max(logits, axis=-1, keepdims=True))
        alpha = jnp.exp(m_prev - m_new)
        p = jnp.exp(logits - m_new)
        l_ref[...] = alpha * l_ref[...] + jnp.sum(p, axis=-1, keepdims=True)
        acc_ref[...] = alpha * acc_ref[...] + pv_fn(p.astype(BF16))
        m_ref[...] = m_new

    carry = f_ref[...]
    f_pages = []
    for i in range(pps):
        f_page = lf_refs[i][0, 0] + carry
        carry = f_page[:, PAGE_SIZE - 1:PAGE_SIZE]
        f_pages.append(jnp.concatenate([f_page] * reps, axis=0))
    f_ref[...] = carry
    k_all = jnp.concatenate([r[0, 0].reshape(w, PAGE_SIZE).astype(BF16) for r in k_refs], axis=1)
    s = jnp.dot(qbd, k_all, preferred_element_type=F32)

    def pv_pages(pb):
        v_all = jnp.concatenate([r[0, 0].reshape(w, PAGE_SIZE).astype(BF16) for r in v_refs], axis=1)
        return lax.dot_general(pb, v_all, (((1,), (1,)), ((), ())), preferred_element_type=F32)

    update(s - jnp.concatenate(f_pages, axis=1), pv_pages)

    @pl.when(step == pl.num_programs(1) - 1)
    def _():
        pad = jnp.zeros((PAGE_SIZE - knew_ref.shape[1], w), F32)
        k_new = jnp.concatenate([knew_ref[0], pad], axis=0).astype(BF16)
        v_new = jnp.concatenate([vnew_ref[0], pad], axis=0).astype(BF16)
        f_new = _lane_cumsum(lfnew_ref[0]) + carry
        lg = lax.dot_general(qbd, k_new, (((1,), (1,)), ((), ())), preferred_element_type=F32)
        lg = lg - jnp.concatenate([f_new] * reps, axis=0)
        row = lax.broadcasted_iota(jnp.int32, lg.shape, 0)
        col = lax.broadcasted_iota(jnp.int32, lg.shape, 1)
        lg = jnp.where(col <= row // FOX_HEADS, lg, -jnp.inf)
        update(lg, lambda pb: jnp.dot(pb, v_new, preferred_element_type=F32))
        o = acc_ref[...] / l_ref[...]
        row = lax.broadcasted_iota(jnp.int32, o.shape, 0)
        col = lax.broadcasted_iota(jnp.int32, o.shape, 1)
        o = jnp.where(col // FOX_HEAD_DIM == row % FOX_HEADS, o, 0.0)
        o_ref[0] = jnp.concatenate(
            [jnp.sum(o[j * FOX_HEADS:(j + 1) * FOX_HEADS], axis=0, keepdims=True)
             for j in range(reps)], axis=0)


def fox_sample_attention(page_table, qbd, k_new, v_new, lf_new_t, k_cache_t, v_cache_t, lf_cache_t, pps):
    b, nq, w = qbd.shape
    n_pages = page_table.shape[1]
    pt = page_table.reshape(-1)
    s_new = nq // FOX_HEADS

    def page_map(i, nd):
        return lambda bi, s, pt_ref: (0, pt_ref[bi * n_pages + s * pps + i]) + (0,) * nd

    def bmap(bi, s, pt_ref):
        return (bi, 0, 0)

    kv_blk = (1, 1, FOX_HEADS, FOX_HEAD_DIM, PAGE_SIZE)
    in_specs = [pl.BlockSpec((1, nq, w), bmap),
                pl.BlockSpec((1,) + k_new.shape[1:], bmap),
                pl.BlockSpec((1,) + v_new.shape[1:], bmap),
                pl.BlockSpec((1,) + lf_new_t.shape[1:], bmap)]
    in_specs += [pl.BlockSpec(kv_blk, page_map(i, 3)) for i in range(pps)]
    in_specs += [pl.BlockSpec(kv_blk, page_map(i, 3)) for i in range(pps)]
    in_specs += [pl.BlockSpec((1, 1, FOX_HEADS, PAGE_SIZE), page_map(i, 2)) for i in range(pps)]
    grid_spec = pltpu.PrefetchScalarGridSpec(
        num_scalar_prefetch=1,
        grid=(b, n_pages // pps),
        in_specs=in_specs,
        out_specs=pl.BlockSpec((1, s_new, w), bmap),
        scratch_shapes=[pltpu.VMEM((nq, 1), F32), pltpu.VMEM((nq, 1), F32),
                        pltpu.VMEM((nq, w), F32), pltpu.VMEM((FOX_HEADS, 1), F32)],
    )
    return pl.pallas_call(
        functools.partial(_fox_sample_kernel, pps=pps),
        out_shape=jax.ShapeDtypeStruct((b, s_new, w), F32),
        grid_spec=grid_spec,
        compiler_params=_cparams(("arbitrary", "arbitrary")),
        name="fox_sample",
    )(pt, qbd, k_new, v_new, lf_new_t, *([k_cache_t] * pps), *([v_cache_t] * pps),
      *([lf_cache_t] * pps))


def _group_mean_sq(x, bd):
    sq = x * x
    hi = sq.astype(BF16)
    lo = (sq - hi.astype(F32)).astype(BF16)
    return (jnp.dot(hi, bd, preferred_element_type=F32) + jnp.dot(lo, bd, preferred_element_type=F32))


def _combine_kernel(fo_ref, mh_ref, mo_ref, x_ref, g1_ref, sc2_ref, sh2_ref, gfo_ref, gml_ref,
                    gffn_ref, wout_ref, wpq_ref, keys_ref, bd64_ref, bd128_ref,
                    x1_ref, h2_ref, st_ref):
    fo = fo_ref[...]
    a_out = fo * lax.rsqrt(_group_mean_sq(fo, bd64_ref[...]) + NORM_EPS) * gfo_ref[...]
    mh = mh_ref[...]
    mo = mo_ref[...]
    b_out = (mh * lax.rsqrt(_group_mean_sq(mh, bd128_ref[...]) + NORM_EPS) * gml_ref[...]
             / (1.0 + jnp.exp(-mo)))
    cat = jnp.concatenate([a_out, b_out], axis=1).astype(BF16)
    x1 = x_ref[...] + g1_ref[...] * jnp.dot(cat, wout_ref[...], preferred_element_type=F32)
    x1_ref[...] = x1
    y = x1 * lax.rsqrt(jnp.mean(x1 * x1, axis=-1, keepdims=True) + NORM_EPS) * gffn_ref[...]
    h2 = (y * (1.0 + sc2_ref[...]) + sh2_ref[...]).astype(BF16)
    h2_ref[...] = h2
    pq = jnp.dot(h2, wpq_ref[...], preferred_element_type=F32).astype(BF16)
    for hp in range(2 * PEER_HEADS):
        st_ref[hp] = lax.dot_general(keys_ref[hp], pq[:, hp * PEER_HALF:(hp + 1) * PEER_HALF],
                                     (((1,), (1,)), ((), ())), preferred_element_type=F32)


def combine(fox_o, m_h, mo, x, gate1, scale2, shift2, g_fox, g_ml, g_ffn, w_out, w_pq, keys,
            bd64, bd128, tm):
    n, d = x.shape
    mod_rows = gate1.shape[0]
    if mod_rows == 1:
        mod_spec = pl.BlockSpec((1, d), lambda i: (0, 0))
    else:
        mod_spec = pl.BlockSpec((tm, d), lambda i: (i, 0))

    def full(a):
        return pl.BlockSpec(a.shape, lambda i: (0,) * a.ndim)

    s512 = pl.BlockSpec((tm, 512), lambda i: (i, 0))
    sd = pl.BlockSpec((tm, d), lambda i: (i, 0))
    return pl.pallas_call(
        _combine_kernel,
        out_shape=[jax.ShapeDtypeStruct((n, d), F32), jax.ShapeDtypeStruct((n, d), BF16),
                   jax.ShapeDtypeStruct((2 * PEER_HEADS, PEER_N_KEYS, n), F32)],
        grid=(n // tm,),
        in_specs=[s512, s512, s512, sd, mod_spec, mod_spec, mod_spec, full(g_fox), full(g_ml),
                  full(g_ffn), full(w_out), full(w_pq), full(keys), full(bd64), full(bd128)],
        out_specs=[sd, sd, pl.BlockSpec((2 * PEER_HEADS, PEER_N_KEYS, tm), lambda i: (0, 0, i))],
        compiler_params=_cparams(("arbitrary",)),
        name="combine",
    )(fox_o, m_h, mo, x, gate1, scale2, shift2, g_fox, g_ml, g_ffn, w_out, w_pq, keys, bd64, bd128)


def _sort_desc(a):
    a = list(a)
    n = len(a)
    k = 2
    while k <= n:
        j = k // 2
        while j >= 1:
            for i in range(n):
                p = i ^ j
                if p > i:
                    hi, lo = jnp.maximum(a[i], a[p]), jnp.minimum(a[i], a[p])
                    a[i], a[p] = (hi, lo) if (i & k) == 0 else (lo, hi)
            j //= 2
        k *= 2
    return a


def _merge_desc(a):
    a = list(a)
    j = len(a) // 2
    while j >= 1:
        for i in range(len(a)):
            p = i ^ j
            if p > i:
                a[i], a[p] = jnp.maximum(a[i], a[p]), jnp.minimum(a[i], a[p])
        j //= 2
    return a


_PEER_CAND = [(i, j) for i in range(PEER_TOPK) for j in range(PEER_TOPK)
              if (i + 1) * (j + 1) <= PEER_TOPK]


def _top16_of_128(slabs):
    t = _sort_desc(slabs)
    sub = lax.broadcasted_iota(jnp.int32, t[0].shape, 0)
    for bit in (1, 2, 4):
        if bit == 4:
            partner = [pltpu.roll(v, 4, 0) for v in t]
        else:
            low = (sub & bit) == 0
            partner = [jnp.where(low, pltpu.roll(v, 8 - bit, 0), pltpu.roll(v, bit, 0)) for v in t]
        t = _merge_desc([jnp.maximum(t[i], partner[PEER_TOPK - 1 - i]) for i in range(PEER_TOPK)])
    return t


def _kth_candidate_sums(t1, t2):
    c = [t1[i] + t2[j] for (i, j) in _PEER_CAND]
    c += [jnp.full(c[0].shape, NEG_BIG, F32)] * (64 - len(c))
    return _sort_desc(c)


def _topk_kernel(st_ref, u1_ref, u2_ref, th_ref, t1_s, t2_s):
    nslab = PEER_N_KEYS // 8

    def pack(t, t_s, h):
        for i in range(PEER_TOPK):
            t_s[i, h:h + 1, :] = t[i][0:1, :]

    for h in range(PEER_HEADS):
        u1_ref[h] = st_ref[2 * h] * LOG2E
        u2_ref[h] = st_ref[2 * h + 1] * LOG2E
        pack(_top16_of_128([u1_ref[h, 8 * i:8 * i + 8, :] for i in range(nslab)]), t1_s, h)
        pack(_top16_of_128([u2_ref[h, 8 * i:8 * i + 8, :] for i in range(nslab)]), t2_s, h)
    t1 = [t1_s[i] for i in range(PEER_TOPK)]
    c = _kth_candidate_sums(t1, [t2_s[i] for i in range(PEER_TOPK)])
    z = jnp.exp2(c[0] - c[0])
    for r in range(1, PEER_TOPK):
        z = z + jnp.exp2(c[r] - c[0])
    k2 = c[0] + jnp.log2(z)
    for h in range(PEER_HEADS):
        u2_ref[h] = u2_ref[h] - k2[h:h + 1, :]
    c = _kth_candidate_sums(t1, [t2_s[i] - k2 for i in range(PEER_TOPK)])
    th_ref[...] = c[PEER_TOPK - 1]


def peer_topk(st):
    tm = 128
    hp, nk, n = st.shape
    uspec = pl.BlockSpec((PEER_HEADS, nk, tm), lambda i: (0, 0, i))
    ushape = jax.ShapeDtypeStruct((PEER_HEADS, nk, n), F32)
    return pl.pallas_call(
        _topk_kernel,
        out_shape=[ushape, ushape, jax.ShapeDtypeStruct((PEER_HEADS, n), F32)],
        grid=(n // tm,),
        in_specs=[pl.BlockSpec((hp, nk, tm), lambda i: (0, 0, i))],
        out_specs=[uspec, uspec, pl.BlockSpec((PEER_HEADS, tm), lambda i: (0, i))],
        scratch_shapes=[pltpu.VMEM((PEER_TOPK, 8, 128), F32), pltpu.VMEM((PEER_TOPK, 8, 128), F32)],
        compiler_params=_cparams(("arbitrary",)),
        name="peer_topk",
    )(st)


def _gelu(x):
    return 0.5 * x * (1.0 + lax.erf(x * (2.0 ** -0.5)))


def _peer_kernel(u1_ref, u2_ref, th_ref, h2_ref, u_ref, vt_ref, x1_ref, g2_ref, gf_ref, o_ref,
                 acc_ref, at_ref, p_ref, *, tm, te):
    e = pl.program_id(1)

    @pl.when(e == 0)
    def _():
        acc_ref[...] = jnp.zeros(acc_ref.shape, F32)

    at_ref[...] = lax.dot_general(u_ref[...], h2_ref[...], (((1,), (1,)), ((), ())),
                                  preferred_element_type=F32)
    a_per_blk = te // PEER_N_KEYS

    heads = range(PEER_HEADS)

    def weighted(r8, th8, rows, lanes):
        w = None
        for h in heads:
            c = r8[h] + u2_ref[h, rows.start % PEER_N_KEYS:rows.start % PEER_N_KEYS + 8, lanes]
            e = jnp.where(c >= th8[h], jnp.exp2(c), 0.0)
            w = e if w is None else w + e
        return w * _gelu(at_ref[rows, lanes])

    for tg in range(tm // 128):
        lanes = slice(tg * 128, (tg + 1) * 128)
        th8 = [jnp.broadcast_to(th_ref[h:h + 1, lanes], (8, 128)) for h in heads]
        for al in range(a_per_blk):
            r8 = [jnp.broadcast_to(u1_ref[h, al:al + 1, lanes], (8, 128)) for h in heads]
            for i in range(0, PEER_N_KEYS, 16):
                r0 = al * PEER_N_KEYS + i
                pair = [weighted(r8, th8, slice(r0 + d, r0 + d + 8), lanes) for d in (0, 8)]
                p_ref[r0:r0 + 16, lanes] = jnp.concatenate(pair, axis=0).astype(BF16)
    acc_ref[...] += jnp.dot(vt_ref[...], p_ref[...], preferred_element_type=F32)

    @pl.when(e == pl.num_programs(1) - 1)
    def _():
        xo = x1_ref[...] + g2_ref[...] * acc_ref[...].T
        o_ref[...] = xo * lax.rsqrt(jnp.mean(xo * xo, axis=-1, keepdims=True) + NORM_EPS) * gf_ref[...]


def peer_experts(u1, u2, th, h2, u_bf, vt_bf, x1, gate2, g_final, tm, te):
    n, d = x1.shape
    n_exp = u_bf.shape[0]
    mod_rows = gate2.shape[0]
    if mod_rows == 1:
        mod_spec = pl.BlockSpec((1, d), lambda i, e: (0, 0))
    else:
        mod_spec = pl.BlockSpec((tm, d), lambda i, e: (i, 0))
    uspec = pl.BlockSpec((PEER_HEADS, PEER_N_KEYS, tm), lambda i, e: (0, 0, i))
    u1spec = pl.BlockSpec((PEER_HEADS, te // PEER_N_KEYS, tm), lambda i, e: (0, e, i))
    return pl.pallas_call(
        functools.partial(_peer_kernel, tm=tm, te=te),
        out_shape=jax.ShapeDtypeStruct((n, d), F32),
        grid=(n // tm, n_exp // te),
        in_specs=[u1spec, uspec, pl.BlockSpec((PEER_HEADS, tm), lambda i, e: (0, i)),
                  pl.BlockSpec((tm, d), lambda i, e: (i, 0)),
                  pl.BlockSpec((te, d), lambda i, e: (e, 0)),
                  pl.BlockSpec((d, te), lambda i, e: (0, e)),
                  pl.BlockSpec((tm, d), lambda i, e: (i, 0)), mod_spec,
                  pl.BlockSpec((1, d), lambda i, e: (0, 0))],
        out_specs=pl.BlockSpec((tm, d), lambda i, e: (i, 0)),
        scratch_shapes=[pltpu.VMEM((d, tm), F32), pltpu.VMEM((te, tm), F32),
                        pltpu.VMEM((te, tm), BF16)],
        compiler_params=_cparams(("arbitrary", "arbitrary")),
        name="peer_experts",
    )(u1, u2, th, h2, u_bf, vt_bf, x1, gate2, g_final)


def _fox_prompt(fq, fk, fv, logf_t, t_blk):
    t = fq.shape[0]
    nh, dh = FOX_HEADS, FOX_HEAD_DIM
    f2, hi, mid, lo = forget_cumsum(logf_t, t_blk)

    def heads(a):
        return a.astype(BF16).reshape(t, nh, dh).transpose(1, 0, 2)

    def heads_t(a):
        return a.astype(BF16).reshape(t, nh, dh).transpose(1, 2, 0)

    qa = jnp.concatenate([heads_t(fq * (dh ** -0.5 * LOG2E)), jnp.full((nh, 3, t), -1.0, BF16),
                          jnp.zeros((nh, 128 - dh - 3, t), BF16)], axis=1)
    ka = jnp.concatenate([heads(fk), jnp.stack([hi, mid, lo], axis=-1).astype(BF16),
                          jnp.full((nh, t, 3), -1.0, BF16),
                          jnp.zeros((nh, t, 128 - dh - 6), BF16)], axis=-1)
    vt = heads_t(fv)
    o_t = fox_prompt_attention(qa, ka, vt, f2[:, ::t_blk].reshape(-1), t_blk)
    return o_t.reshape(nh * dh, t).T


def _prep_weights(w_in, b_fox_f, b_mlstm_i, b_mlstm_f):
    fw, mw = FOX_WIDTH, MLSTM_WIDTH
    o = 3 * fw + FOX_HEADS
    g0 = o + 3 * mw
    wm = jnp.concatenate([w_in[:, :3 * fw], w_in[:, o:o + 3 * mw], w_in[:, g0 + 2 * MLSTM_HEADS:]],
                         axis=1).astype(BF16)
    wg = jnp.concatenate([w_in[:, 3 * fw:o], w_in[:, g0:g0 + 2 * MLSTM_HEADS]], axis=1)
    wg = jnp.pad(wg, ((0, 0), (0, GATE_LANES - wg.shape[1])))
    bg = jnp.concatenate([b_fox_f, b_mlstm_i, b_mlstm_f])
    bg = jnp.pad(bg, (0, GATE_LANES - bg.shape[0])).reshape(1, GATE_LANES)
    return wm, wg, bg


def kernel(x_prompt, x_sample, c_prompt, c_sample, cache_fox_k, cache_fox_v, cache_fox_logf, state_mlstm_C, state_mlstm_n, state_mlstm_m, page_table, w_ada, b_ada, g_norm_mix, g_norm_ffn, w_in, b_fox_f, b_mlstm_i, b_mlstm_f, g_fox_out, g_mlstm_out, w_out, w_peer_q, peer_keys, peer_u, peer_v, g_final):
    D = D_MODEL
    T = x_prompt.shape[1]
    B, S = x_sample.shape[0], x_sample.shape[1]
    n_pages = page_table.shape[1]
    H, DH = FOX_HEADS, FOX_HEAD_DIM
    MH, DK = MLSTM_HEADS, MLSTM_HEAD_DIM
    SP = 8
    SEG_PER_BLK = 128 // SP

    wm, wg, bg = _prep_weights(w_in[0], b_fox_f[0], b_mlstm_i[0], b_mlstm_f[0])
    w_out_b = w_out[0].astype(BF16)
    w_pq_b = w_peer_q[0].astype(BF16)
    keys_b = peer_keys[0].reshape(2 * PEER_HEADS, PEER_N_KEYS, PEER_HALF).astype(BF16)
    u_b = peer_u[0].astype(BF16)
    vt_b = peer_v[0].T.astype(BF16)
    lane = jnp.arange(FOX_WIDTH)
    bd64 = jnp.where(lane[:, None] // DH == lane[None, :] // DH, 1.0 / DH, 0.0).astype(BF16)
    bd128 = jnp.where(lane[:, None] // DK == lane[None, :] // DK, 1.0 / DK, 0.0).astype(BF16)
    g_mix = g_norm_mix[0].reshape(1, D)
    g_ffn = g_norm_ffn[0].reshape(1, D)
    g_fox = g_fox_out[0].reshape(1, FOX_WIDTH)
    g_ml = g_mlstm_out[0].reshape(1, MLSTM_WIDTH)
    g_fin = g_final.reshape(1, D)

    c_all = jnp.concatenate([c_prompt, c_sample], axis=0)
    c_all = jnp.pad(c_all, ((0, (-c_all.shape[0]) % 8), (0, 0)))
    ada = adaln_terms(c_all, w_ada[0], b_ada[0])

    def terms(a):
        return [a[:, i * D:(i + 1) * D] for i in range(6)]

    shift1_p, scale1_p, gate1_p, shift2_p, scale2_p, gate2_p = terms(ada[0:1])
    shift1_s, scale1_s, gate1_s, shift2_s, scale2_s, gate2_s = terms(
        jnp.repeat(ada[1:1 + B], S, axis=0))

    xp = x_prompt.reshape(T, D)
    fq, fk, fv, mq, mk, mv, mo, gt, gtt = in_projection(xp, scale1_p, shift1_p, g_mix, wm, wg, bg, 256)
    fox_o = _fox_prompt(fq, fk, fv, gtt[0:H], 512)
    m_h, c_p, n_p, m_p = mlstm(
        mq.reshape(1, T, MLSTM_WIDTH), mk.reshape(1, T, MLSTM_WIDTH), mv.reshape(1, T, MLSTM_WIDTH),
        gt.reshape(1, T, GATE_LANES), gtt[0:16].reshape(1, 16, T),
        jnp.zeros((1, MH, DK, DK), F32), jnp.zeros((1, MH, 1, DK), F32), jnp.zeros((1, MH, 1, 1), F32),
        256, 256)
    x1, h2, st = combine(fox_o, m_h.reshape(T, MLSTM_WIDTH), mo, xp, gate1_p, scale2_p, shift2_p,
                         g_fox, g_ml, g_ffn, w_out_b, w_pq_b, keys_b, bd64, bd128, 256)
    u1, u2, th = peer_topk(st)
    y_p = peer_experts(u1, u2, th, h2, u_b, vt_b, x1, gate2_p, g_fin, 512, 1024)

    ns = B * S
    xs = x_sample.reshape(ns, D)
    sfq, sfk, sfv, smq, smk, smv, smo, sgt, _ = in_projection(
        xs, scale1_s, shift1_s, g_mix, wm, wg, bg, 256)
    def pad_tok(a):
        a = a.reshape(B, S, a.shape[-1])
        return jnp.pad(a, ((0, 0), (0, SP - S), (0, 0)))

    q4 = (sfq * DH ** -0.5).reshape(B, S, H, 1, DH)
    qbd = (q4 * jnp.eye(H, dtype=F32)[None, None, :, :, None]).astype(BF16).reshape(B, S * H, FOX_WIDTH)
    lf_new_t = jnp.pad(sgt[:, 0:H].reshape(B, S, H).transpose(0, 2, 1),
                       ((0, 0), (0, 0), (0, PAGE_SIZE - S)))
    n_pool = cache_fox_k.shape[1]
    lf_pages = rows_cumsum(cache_fox_logf.transpose(0, 1, 3, 2).reshape(n_pool * H, PAGE_SIZE), 8192)
    so = fox_sample_attention(
        page_table, qbd, pad_tok(sfk), pad_tok(sfv), lf_new_t,
        cache_fox_k.transpose(0, 1, 3, 4, 2), cache_fox_v.transpose(0, 1, 3, 4, 2),
        lf_pages.reshape(1, n_pool, H, PAGE_SIZE), 16)
    pad_gate = jnp.zeros((GATE_LANES,), F32).at[H:H + MH].set(NEG_BIG)
    sg = jnp.concatenate([sgt.reshape(B, S, GATE_LANES),
                          jnp.broadcast_to(pad_gate, (B, SP - S, GATE_LANES))], axis=1)
    sg = sg.reshape(B // SEG_PER_BLK, 128, GATE_LANES)

    def blk(a):
        return pad_tok(a).reshape(B // SEG_PER_BLK, 128, a.shape[-1])

    sm_h, c_s, n_s, m_s = mlstm(
        blk(smq), blk(smk), blk(smv), sg, sg.transpose(0, 2, 1)[:, 0:16],
        state_mlstm_C[0], state_mlstm_n[0].reshape(B, MH, 1, DK), state_mlstm_m[0].reshape(B, MH, 1, 1),
        128, SP)
    sm_h = sm_h.reshape(B, SP, MLSTM_WIDTH)[:, 0:S].reshape(ns, MLSTM_WIDTH)
    sx1, sh2, sst = combine(so.reshape(ns, FOX_WIDTH), sm_h, smo, xs, gate1_s, scale2_s, shift2_s,
                            g_fox, g_ml, g_ffn, w_out_b, w_pq_b, keys_b, bd64, bd128, 256)
    su1, su2, sth = peer_topk(sst)
    y_s = peer_experts(su1, su2, sth, sh2, u_b, vt_b, sx1, gate2_s, g_fin, 512, 1024)

    return (y_p.reshape(1, T, D), y_s.reshape(B, S, D),
            fk.reshape(1, 1, T, H, DH), fv.reshape(1, 1, T, H, DH), gt[:, 0:H].reshape(1, 1, T, H),
            c_p.reshape(1, 1, MH, DK, DK), n_p.reshape(1, 1, MH, DK), m_p.reshape(1, 1, MH),
            sfk.reshape(1, B, S, H, DH), sfv.reshape(1, B, S, H, DH), sgt[:, 0:H].reshape(1, B, S, H),
            c_s.reshape(1, B, MH, DK, DK), n_s.reshape(1, B, MH, DK), m_s.reshape(1, B, MH))
```

```python
import functools
import math

import jax
import jax.numpy as jnp
from jax import lax
from jax.experimental import pallas as pl
from jax.experimental.pallas import tpu as pltpu

F32 = jnp.float32
BF16 = jnp.bfloat16
HIGHEST = lax.Precision.HIGHEST

D_MODEL = 1024
FOX_HEADS = 8
FOX_HEAD_DIM = 64
FOX_WIDTH = FOX_HEADS * FOX_HEAD_DIM
MLSTM_HEADS = 4
MLSTM_HEAD_DIM = 128
MLSTM_WIDTH = MLSTM_HEADS * MLSTM_HEAD_DIM
PAGE_SIZE = 128
PEER_HEADS = 8
PEER_N_KEYS = 128
PEER_TOPK = 16
PEER_HALF = 128
NORM_EPS = 1e-6
GATE_LANES = 128
NEG_BIG = -1e30
LOG2E = 1.4426950408889634
VMEM_LIMIT = 56 * 1024 * 1024


def _cparams(sem, flags=None):
    return pltpu.CompilerParams(dimension_semantics=sem, vmem_limit_bytes=VMEM_LIMIT, flags=flags)


def _log_sigmoid(x):
    return jnp.minimum(x, 0.0) - jnp.log1p(jnp.exp(-jnp.abs(x)))


def _adaln_kernel(c_ref, w_ref, b_ref, o_ref):
    c = c_ref[...]
    s = c / (1.0 + jnp.exp(-c))
    o_ref[...] = jnp.dot(s, w_ref[...], precision=HIGHEST, preferred_element_type=F32) + b_ref[...]


def adaln_terms(c, w_ada, b_ada):
    rows, d = c.shape
    n = w_ada.shape[1]
    tn = 768
    return pl.pallas_call(
        _adaln_kernel,
        out_shape=jax.ShapeDtypeStruct((rows, n), F32),
        grid=(n // tn,),
        in_specs=[pl.BlockSpec((rows, d), lambda j: (0, 0)),
                  pl.BlockSpec((d, tn), lambda j: (0, j)),
                  pl.BlockSpec((1, tn), lambda j: (0, j))],
        out_specs=pl.BlockSpec((rows, tn), lambda j: (0, j)),
        compiler_params=_cparams(("arbitrary",)),
        name="adaln",
    )(c, w_ada, b_ada.reshape(1, n))


def _inproj_kernel(x_ref, sc_ref, sh_ref, g_ref, wm_ref, wg_ref, bg_ref,
                   fq_ref, fk_ref, fv_ref, mq_ref, mk_ref, mv_ref, mo_ref, gt_ref, gtt_ref):
    x = x_ref[...]
    y = x * lax.rsqrt(jnp.mean(x * x, axis=-1, keepdims=True) + NORM_EPS) * g_ref[...]
    h = y * (1.0 + sc_ref[...]) + sh_ref[...]
    hb = h.astype(BF16)
    for i, o_ref in enumerate((fq_ref, fk_ref, fv_ref, mq_ref, mk_ref, mv_ref, mo_ref)):
        o_ref[...] = jnp.dot(hb, wm_ref[:, i * 512:(i + 1) * 512], preferred_element_type=F32)
    zg = jnp.dot(h, wg_ref[...], precision=HIGHEST, preferred_element_type=F32) + bg_ref[...]
    col = lax.broadcasted_iota(jnp.int32, zg.shape, 1)
    is_i = (col >= FOX_HEADS) & (col < FOX_HEADS + MLSTM_HEADS)
    gt = jnp.where(is_i, zg, _log_sigmoid(zg))
    gt_ref[...] = gt
    gtt_ref[...] = gt.T


def in_projection(x, scale, shift, g, wm, wg, bg, tm):
    n, d = x.shape
    mod_rows = scale.shape[0]
    if mod_rows == 1:
        mod_spec = pl.BlockSpec((1, d), lambda i: (0, 0))
    else:
        mod_spec = pl.BlockSpec((tm, d), lambda i: (i, 0))
    o512 = jax.ShapeDtypeStruct((n, 512), F32)
    s512 = pl.BlockSpec((tm, 512), lambda i: (i, 0))
    return pl.pallas_call(
        _inproj_kernel,
        out_shape=[o512] * 7 + [jax.ShapeDtypeStruct((n, GATE_LANES), F32),
                                jax.ShapeDtypeStruct((GATE_LANES, n), F32)],
        grid=(n // tm,),
        in_specs=[pl.BlockSpec((tm, d), lambda i: (i, 0)), mod_spec, mod_spec,
                  pl.BlockSpec((1, d), lambda i: (0, 0)),
                  pl.BlockSpec(wm.shape, lambda i: (0, 0)),
                  pl.BlockSpec(wg.shape, lambda i: (0, 0)),
                  pl.BlockSpec((1, GATE_LANES), lambda i: (0, 0))],
        out_specs=[s512] * 7 + [pl.BlockSpec((tm, GATE_LANES), lambda i: (i, 0)),
                                pl.BlockSpec((GATE_LANES, tm), lambda i: (0, i))],
        compiler_params=_cparams(("arbitrary",)),
        name="inproj",
    )(x, scale, shift, g, wm, wg, bg)


def _lane_cumsum(x, stride=1):
    n = x.shape[-1]
    lane = lax.broadcasted_iota(jnp.int32, x.shape, x.ndim - 1)
    sh = stride
    while sh < n:
        x = x + jnp.where(lane >= sh, pltpu.roll(x, sh, x.ndim - 1), 0.0)
        sh *= 2
    return x


def _cumsum_kernel(x_ref, f_ref, hi_ref, mid_ref, lo_ref, *, blk):
    f = _lane_cumsum(x_ref[...]) * LOG2E
    f_ref[...] = f
    for b in range(f.shape[-1] // blk):
        cols = slice(b * blk, (b + 1) * blk)
        rel = f[:, cols] - f[:, b * blk:b * blk + 1]
        hi = rel.astype(BF16).astype(F32)
        mid = (rel - hi).astype(BF16).astype(F32)
        hi_ref[:, cols] = hi
        mid_ref[:, cols] = mid
        lo_ref[:, cols] = (rel - hi - mid).astype(BF16).astype(F32)


def forget_cumsum(x, blk):
    out = jax.ShapeDtypeStruct(x.shape, F32)
    return pl.pallas_call(
        functools.partial(_cumsum_kernel, blk=blk),
        out_shape=[out, out, out, out],
        name="cumsum",
    )(x)


def _rows_cumsum_kernel(x_ref, o_ref):
    o_ref[...] = _lane_cumsum(x_ref[...])


def rows_cumsum(x, tr):
    rows, n = x.shape
    return pl.pallas_call(
        _rows_cumsum_kernel,
        out_shape=jax.ShapeDtypeStruct(x.shape, F32),
        grid=(rows // tr,),
        in_specs=[pl.BlockSpec((tr, n), lambda i: (i, 0))],
        out_specs=pl.BlockSpec((tr, n), lambda i: (i, 0)),
        compiler_params=_cparams(("arbitrary",)),
        name="page_cumsum",
    )(x)


def _flash_kernel(qi_ref, ki_ref, fs_ref, qa_ref, ka_ref, vt_ref, o_ref, m_ref, l_ref, acc_ref, *, nb):
    s_idx = pl.program_id(0)
    qi = qi_ref[s_idx]
    ki = ki_ref[s_idx]
    nh, da, tq = qa_ref.shape
    dh = vt_ref.shape[1]

    @pl.when(ki == 0)
    def _():
        m_ref[...] = jnp.full(m_ref.shape, -jnp.inf, F32)
        l_ref[...] = jnp.zeros(l_ref.shape, F32)
        acc_ref[...] = jnp.zeros(acc_ref.shape, F32)

    def mask(x):
        key = lax.broadcasted_iota(jnp.int32, x.shape, 0)
        qry = lax.broadcasted_iota(jnp.int32, x.shape, 1)
        return jnp.where(key <= qry, x, -jnp.inf)

    def new_max(h, masked):
        raw = jnp.dot(ka_ref[h], qa_ref[h], preferred_element_type=F32)
        if masked:
            raw = mask(raw)
        cb = -fs_ref[h * nb + ki]
        m_prev = m_ref[h]
        m_new = jnp.maximum(m_prev, jnp.max(raw, axis=0, keepdims=True) + cb)
        m_ref[h] = m_new
        x = m_new - cb
        hi = x.astype(BF16).astype(F32)
        mid = (x - hi).astype(BF16).astype(F32)
        lo = (x - hi - mid).astype(BF16).astype(F32)
        row = lax.broadcasted_iota(jnp.int32, (16, tq), 0)
        aug = jnp.where(row < 3, -1.0,
                        jnp.where(row == 3, hi, jnp.where(row == 4, mid, jnp.where(row == 5, lo, 0.0))))
        q2 = jnp.concatenate([qa_ref[h, 0:dh, :], aug.astype(BF16),
                              jnp.zeros((da - dh - 16, tq), BF16)], axis=0)
        return q2, jnp.exp2(m_prev - m_new)

    def probabilities(h, q2, alpha, masked):
        raw = jnp.dot(ka_ref[h], q2, preferred_element_type=F32)
        if masked:
            raw = mask(raw)
        p = jnp.exp2(raw)
        l_ref[h] = alpha * l_ref[h] + jnp.sum(p, axis=0, keepdims=True)
        return p.astype(BF16)

    def accumulate(h, p, alpha):
        acc_ref[h] = alpha * acc_ref[h] + jnp.dot(vt_ref[h], p, preferred_element_type=F32)

    def step(masked):
        nxt = new_max(0, masked)
        pending = None
        for h in range(nh):
            q2, alpha = nxt
            if h + 1 < nh:
                nxt = new_max(h + 1, masked)
            p = probabilities(h, q2, alpha, masked)
            if pending is not None:
                accumulate(*pending)
            pending = (h, p, alpha)
        accumulate(*pending)

    @pl.when(ki < qi)
    def _():
        step(False)

    @pl.when(ki == qi)
    def _():
        step(True)
        o_ref[...] = acc_ref[...] / l_ref[...]


def fox_prompt_attention(qa, ka, vt, f_start, t_blk):
    h, t, da = ka.shape
    dh = vt.shape[1]
    nb = t // t_blk
    pairs = [(i, j) for i in range(nb) for j in range(i + 1)]
    qi_arr = jnp.array([p[0] for p in pairs], jnp.int32)
    ki_arr = jnp.array([p[1] for p in pairs], jnp.int32)
    grid_spec = pltpu.PrefetchScalarGridSpec(
        num_scalar_prefetch=3,
        grid=(len(pairs),),
        in_specs=[pl.BlockSpec((h, da, t_blk), lambda s, qi, ki, fs: (0, 0, qi[s])),
                  pl.BlockSpec((h, t_blk, da), lambda s, qi, ki, fs: (0, ki[s], 0)),
                  pl.BlockSpec((h, dh, t_blk), lambda s, qi, ki, fs: (0, 0, ki[s]))],
        out_specs=pl.BlockSpec((h, dh, t_blk), lambda s, qi, ki, fs: (0, 0, qi[s])),
        scratch_shapes=[pltpu.VMEM((h, 1, t_blk), F32), pltpu.VMEM((h, 1, t_blk), F32),
                        pltpu.VMEM((h, dh, t_blk), F32)],
    )
    return pl.pallas_call(
        functools.partial(_flash_kernel, nb=nb),
        out_shape=jax.ShapeDtypeStruct((h, dh, t), F32),
        grid_spec=grid_spec,
        compiler_params=_cparams(("arbitrary",)),
        name="fox_prompt",
    )(qi_arr, ki_arr, f_start, qa, ka, vt)


def _mlstm_kernel(q_ref, k_ref, v_ref, g_ref, gt_ref, c0_ref, n0_ref, m0_ref,
                  h_ref, c_out_ref, n_out_ref, m_out_ref, c_s, n_s, m_s, *, rows, seg_len):
    nseg = rows // seg_len
    dk = MLSTM_HEAD_DIM
    c_idx = pl.program_id(1)

    @pl.when(c_idx == 0)
    def _():
        c_s[...] = c0_ref[...]
        n_s[...] = n0_ref[...]
        m_s[...] = m0_ref[...]

    g = g_ref[0]
    gt = gt_ref[0]
    row = lax.broadcasted_iota(jnp.int32, (rows, rows), 0)
    col = lax.broadcasted_iota(jnp.int32, (rows, rows), 1)
    if nseg == 1:
        causal = col <= row
        anti = row <= col
    else:
        same = (row // seg_len) == (col // seg_len)
        causal = (col <= row) & same
        anti = (row <= col) & same
    rid = lax.broadcasted_iota(jnp.int32, (rows, 1), 0) // seg_len

    def per_row(vals):
        out = jnp.broadcast_to(vals[0], (rows, 1))
        for j in range(1, nseg):
            out = jnp.where(rid == j, vals[j], out)
        return out

    for h in range(MLSTM_HEADS):
        lane = slice(h * dk, (h + 1) * dk)
        q = q_ref[0, :, lane]
        k = k_ref[0, :, lane] * (dk ** -0.5)
        v = v_ref[0, :, lane]
        i_row = gt[8 + h:9 + h, :]
        lf_row = gt[12 + h:13 + h, :]
        i_col = g[:, 8 + h:9 + h]
        lf_col = g[:, 12 + h:13 + h]
        b_col = jnp.sum(jnp.where(causal, lf_row, 0.0), axis=1, keepdims=True)
        b_row = jnp.sum(jnp.where(anti, lf_col, 0.0), axis=0, keepdims=True)
        log_d = jnp.where(causal, b_col - b_row + i_row, -jnp.inf)
        m_prev = [m_s[j, h] for j in range(nseg)]
        m_inter = b_col + per_row(m_prev)
        m_t = jnp.maximum(m_inter, jnp.max(log_d, axis=1, keepdims=True))
        d_mat = jnp.exp(log_d - m_t)
        inter = jnp.exp(m_inter - m_t)
        qb = q.astype(BF16)
        s_mat = lax.dot_general(qb, k.astype(BF16), (((1,), (1,)), ((), ())),
                                preferred_element_type=F32) * d_mat
        num = jnp.dot(s_mat.astype(BF16), v.astype(BF16), preferred_element_type=F32)
        qc = jnp.concatenate(
            [jnp.dot(qb[j * seg_len:(j + 1) * seg_len], c_s[j, h].astype(BF16),
                     preferred_element_type=F32) for j in range(nseg)], axis=0)
        n_rows = jnp.concatenate(
            [jnp.broadcast_to(n_s[j, h], (seg_len, dk)) for j in range(nseg)], axis=0)
        qn = jnp.sum(q * n_rows, axis=1, keepdims=True)
        num = inter * qc + num
        den = inter * qn + jnp.sum(s_mat, axis=1, keepdims=True)
        h_ref[0, :, lane] = num / jnp.maximum(jnp.abs(den), jnp.exp(-m_t))

        last = [(j + 1) * seg_len - 1 for j in range(nseg)]
        b_last = [b_col[r:r + 1, :] for r in last]
        m_new = [m_t[r:r + 1, :] for r in last]
        w_end = jnp.exp(per_row(b_last) - b_col + i_col - per_row(m_new))
        kw = k * w_end
        kwt = kw.T.astype(BF16)
        vb = v.astype(BF16)
        if nseg > 1:
            vb = jnp.concatenate([jnp.where(rid == j, vb, jnp.zeros_like(vb))
                                  for j in range(nseg)], axis=1)
        upd = jnp.dot(kwt, vb, preferred_element_type=F32)
        for j in range(nseg):
            decay = jnp.exp(b_last[j] + m_prev[j] - m_new[j])
            kw_j = kw if nseg == 1 else jnp.where(rid == j, kw, 0.0)
            c_s[j, h] = decay * c_s[j, h] + upd[:, j * dk:(j + 1) * dk]
            n_s[j, h] = decay * n_s[j, h] + jnp.sum(kw_j, axis=0, keepdims=True)
            m_s[j, h] = m_new[j]

    @pl.when(c_idx == pl.num_programs(1) - 1)
    def _():
        c_out_ref[...] = c_s[...]
        n_out_ref[...] = n_s[...]
        m_out_ref[...] = m_s[...]


def mlstm(q, k, v, gates, gates_t, c0, n0, m0, rows, seg_len):
    G, tg, w = q.shape
    nseg = rows // seg_len
    nc = tg // rows
    assert nseg == 1 or nc == 1
    qspec = pl.BlockSpec((1, rows, w), lambda gi, c: (gi, c, 0))
    cspec = pl.BlockSpec((nseg, MLSTM_HEADS, 128, 128), lambda gi, c: (gi, 0, 0, 0))
    nspec = pl.BlockSpec((nseg, MLSTM_HEADS, 1, 128), lambda gi, c: (gi, 0, 0, 0))
    mspec = pl.BlockSpec((nseg, MLSTM_HEADS, 1, 1), lambda gi, c: (gi, 0, 0, 0))
    return pl.pallas_call(
        functools.partial(_mlstm_kernel, rows=rows, seg_len=seg_len),
        out_shape=[jax.ShapeDtypeStruct((G, tg, w), F32),
                   jax.ShapeDtypeStruct(c0.shape, F32),
                   jax.ShapeDtypeStruct(n0.shape, F32),
                   jax.ShapeDtypeStruct(m0.shape, F32)],
        grid=(G, nc),
        in_specs=[qspec, qspec, qspec,
                  pl.BlockSpec((1, rows, GATE_LANES), lambda gi, c: (gi, c, 0)),
                  pl.BlockSpec((1, 16, rows), lambda gi, c: (gi, 0, c)),
                  cspec, nspec, mspec],
        out_specs=[qspec, cspec, nspec, mspec],
        scratch_shapes=[pltpu.VMEM((nseg, MLSTM_HEADS, 128, 128), F32),
                        pltpu.VMEM((nseg, MLSTM_HEADS, 1, 128), F32),
                        pltpu.VMEM((nseg, MLSTM_HEADS, 1, 1), F32)],
        compiler_params=_cparams(("arbitrary", "arbitrary")),
        name="mlstm",
    )(q, k, v, gates, gates_t, c0, n0, m0)


def _fox_sample_kernel(pt_ref, qbd_ref, knew_ref, vnew_ref, lfnew_ref, *rest, pps):
    k_refs = rest[0:pps]
    v_refs = rest[pps:2 * pps]
    lf_refs = rest[2 * pps:3 * pps]
    o_ref = rest[3 * pps]
    m_ref, l_ref, acc_ref, f_ref = rest[3 * pps + 1:]
    step = pl.program_id(1)
    nq, w = qbd_ref.shape[1], qbd_ref.shape[2]
    reps = nq // FOX_HEADS

    @pl.when(step == 0)
    def _():
        m_ref[...] = jnp.full(m_ref.shape, -jnp.inf, F32)
        l_ref[...] = jnp.zeros(l_ref.shape, F32)
        acc_ref[...] = jnp.zeros(acc_ref.shape, F32)
        f_ref[...] = jnp.zeros(f_ref.shape, F32)

    qbd = qbd_ref[0]

    def update(logits, pv_fn):
        m_prev = m_ref[...]
        m_new = jnp.maximum(m_prev, jnp.max(logits, axis=-1, keepdims=True))
        alpha = jnp.exp(m_prev - m_new)
        p = jnp.exp(logits - m_new)
        l_ref[...] = alpha * l_ref[...] + jnp.sum(p, axis=-1, keepdims=True)
        acc_ref[...] = alpha * acc_ref[...] + pv_fn(p.astype(BF16))
        m_ref[...] = m_new

    carry = f_ref[...]
    f_pages = []
    for i in range(pps):
        f_page = lf_refs[i][0, 0] + carry
        carry = f_page[:, PAGE_SIZE - 1:PAGE_SIZE]
        f_pages.append(jnp.concatenate([f_page] * reps, axis=0))
    f_ref[...] = carry
    k_all = jnp.concatenate([r[0, 0].reshape(w, PAGE_SIZE).astype(BF16) for r in k_refs], axis=1)
    s = jnp.dot(qbd, k_all, preferred_element_type=F32)

    def pv_pages(pb):
        v_all = jnp.concatenate([r[0, 0].reshape(w, PAGE_SIZE).astype(BF16) for r in v_refs], axis=1)
        return lax.dot_general(pb, v_all, (((1,), (1,)), ((), ())), preferred_element_type=F32)

    update(s - jnp.concatenate(f_pages, axis=1), pv_pages)

    @pl.when(step == pl.num_programs(1) - 1)
    def _():
        pad = jnp.zeros((PAGE_SIZE - knew_ref.shape[1], w), F32)
        k_new = jnp.concatenate([knew_ref[0], pad], axis=0).astype(BF16)
        v_new = jnp.concatenate([vnew_ref[0], pad], axis=0).astype(BF16)
        f_new = _lane_cumsum(lfnew_ref[0]) + carry
        lg = lax.dot_general(qbd, k_new, (((1,), (1,)), ((), ())), preferred_element_type=F32)
        lg = lg - jnp.concatenate([f_new] * reps, axis=0)
        row = lax.broadcasted_iota(jnp.int32, lg.shape, 0)
        col = lax.broadcasted_iota(jnp.int32, lg.shape, 1)
        lg = jnp.where(col <= row // FOX_HEADS, lg, -jnp.inf)
        update(lg, lambda pb: jnp.dot(pb, v_new, preferred_element_type=F32))
        o = acc_ref[...] / l_ref[...]
        row = lax.broadcasted_iota(jnp.int32, o.shape, 0)
        col = lax.broadcasted_iota(jnp.int32, o.shape, 1)
        o = jnp.where(col // FOX_HEAD_DIM == row % FOX_HEADS, o, 0.0)
        o_ref[0] = jnp.concatenate(
            [jnp.sum(o[j * FOX_HEADS:(j + 1) * FOX_HEADS], axis=0, keepdims=True)
             for j in range(reps)], axis=0)


def fox_sample_attention(page_table, qbd, k_new, v_new, lf_new_t, k_cache_t, v_cache_t, lf_cache_t, pps):
    b, nq, w = qbd.shape
    n_pages = page_table.shape[1]
    pt = page_table.reshape(-1)
    s_new = nq // FOX_HEADS

    def page_map(i, nd):
        return lambda bi, s, pt_ref: (0, pt_ref[bi * n_pages + s * pps + i]) + (0,) * nd

    def bmap(bi, s, pt_ref):
        return (bi, 0, 0)

    kv_blk = (1, 1, FOX_HEADS, FOX_HEAD_DIM, PAGE_SIZE)
    in_specs = [pl.BlockSpec((1, nq, w), bmap),
                pl.BlockSpec((1,) + k_new.shape[1:], bmap),
                pl.BlockSpec((1,) + v_new.shape[1:], bmap),
                pl.BlockSpec((1,) + lf_new_t.shape[1:], bmap)]
    in_specs += [pl.BlockSpec(kv_blk, page_map(i, 3)) for i in range(pps)]
    in_specs += [pl.BlockSpec(kv_blk, page_map(i, 3)) for i in range(pps)]
    in_specs += [pl.BlockSpec((1, 1, FOX_HEADS, PAGE_SIZE), page_map(i, 2)) for i in range(pps)]
    grid_spec = pltpu.PrefetchScalarGridSpec(
        num_scalar_prefetch=1,
        grid=(b, n_pages // pps),
        in_specs=in_specs,
        out_specs=pl.BlockSpec((1, s_new, w), bmap),
        scratch_shapes=[pltpu.VMEM((nq, 1), F32), pltpu.VMEM((nq, 1), F32),
                        pltpu.VMEM((nq, w), F32), pltpu.VMEM((FOX_HEADS, 1), F32)],
    )
    return pl.pallas_call(
        functools.partial(_fox_sample_kernel, pps=pps),
        out_shape=jax.ShapeDtypeStruct((b, s_new, w), F32),
        grid_spec=grid_spec,
        compiler_params=_cparams(("arbitrary", "arbitrary")),
        name="fox_sample",
    )(pt, qbd, k_new, v_new, lf_new_t, *([k_cache_t] * pps), *([v_cache_t] * pps),
      *([lf_cache_t] * pps))


def _group_mean_sq(x, bd):
    sq = x * x
    hi = sq.astype(BF16)
    lo = (sq - hi.astype(F32)).astype(BF16)
    return (jnp.dot(hi, bd, preferred_element_type=F32) + jnp.dot(lo, bd, preferred_element_type=F32))


def _combine_kernel(fo_ref, mh_ref, mo_ref, x_ref, g1_ref, sc2_ref, sh2_ref, gfo_ref, gml_ref,
                    gffn_ref, wout_ref, wpq_ref, keys_ref, bd64_ref, bd128_ref,
                    x1_ref, h2_ref, st_ref):
    fo = fo_ref[...]
    a_out = fo * lax.rsqrt(_group_mean_sq(fo, bd64_ref[...]) + NORM_EPS) * gfo_ref[...]
    mh = mh_ref[...]
    mo = mo_ref[...]
    b_out = (mh * lax.rsqrt(_group_mean_sq(mh, bd128_ref[...]) + NORM_EPS) * gml_ref[...]
             / (1.0 + jnp.exp(-mo)))
    cat = jnp.concatenate([a_out, b_out], axis=1).astype(BF16)
    x1 = x_ref[...] + g1_ref[...] * jnp.dot(cat, wout_ref[...], preferred_element_type=F32)
    x1_ref[...] = x1
    y = x1 * lax.rsqrt(jnp.mean(x1 * x1, axis=-1, keepdims=True) + NORM_EPS) * gffn_ref[...]
    h2 = (y * (1.0 + sc2_ref[...]) + sh2_ref[...]).astype(BF16)
    h2_ref[...] = h2
    pq = jnp.dot(h2, wpq_ref[...], preferred_element_type=F32).astype(BF16)
    for hp in range(2 * PEER_HEADS):
        st_ref[hp] = lax.dot_general(keys_ref[hp], pq[:, hp * PEER_HALF:(hp + 1) * PEER_HALF],
                                     (((1,), (1,)), ((), ())), preferred_element_type=F32)


def combine(fox_o, m_h, mo, x, gate1, scale2, shift2, g_fox, g_ml, g_ffn, w_out, w_pq, keys,
            bd64, bd128, tm):
    n, d = x.shape
    mod_rows = gate1.shape[0]
    if mod_rows == 1:
        mod_spec = pl.BlockSpec((1, d), lambda i: (0, 0))
    else:
        mod_spec = pl.BlockSpec((tm, d), lambda i: (i, 0))

    def full(a):
        return pl.BlockSpec(a.shape, lambda i: (0,) * a.ndim)

    s512 = pl.BlockSpec((tm, 512), lambda i: (i, 0))
    sd = pl.BlockSpec((tm, d), lambda i: (i, 0))
    return pl.pallas_call(
        _combine_kernel,
        out_shape=[jax.ShapeDtypeStruct((n, d), F32), jax.ShapeDtypeStruct((n, d), BF16),
                   jax.ShapeDtypeStruct((2 * PEER_HEADS, PEER_N_KEYS, n), F32)],
        grid=(n // tm,),
        in_specs=[s512, s512, s512, sd, mod_spec, mod_spec, mod_spec, full(g_fox), full(g_ml),
                  full(g_ffn), full(w_out), full(w_pq), full(keys), full(bd64), full(bd128)],
        out_specs=[sd, sd, pl.BlockSpec((2 * PEER_HEADS, PEER_N_KEYS, tm), lambda i: (0, 0, i))],
        compiler_params=_cparams(("arbitrary",)),
        name="combine",
    )(fox_o, m_h, mo, x, gate1, scale2, shift2, g_fox, g_ml, g_ffn, w_out, w_pq, keys, bd64, bd128)


def _sort_desc(a):
    a = list(a)
    n = len(a)
    k = 2
    while k <= n:
        j = k // 2
        while j >= 1:
            for i in range(n):
                p = i ^ j
                if p > i:
                    hi, lo = jnp.maximum(a[i], a[p]), jnp.minimum(a[i], a[p])
                    a[i], a[p] = (hi, lo) if (i & k) == 0 else (lo, hi)
            j //= 2
        k *= 2
    return a


def _merge_desc(a):
    a = list(a)
    j = len(a) // 2
    while j >= 1:
        for i in range(len(a)):
            p = i ^ j
            if p > i:
                a[i], a[p] = jnp.maximum(a[i], a[p]), jnp.minimum(a[i], a[p])
        j //= 2
    return a


_PEER_CAND = [(i, j) for i in range(PEER_TOPK) for j in range(PEER_TOPK)
              if (i + 1) * (j + 1) <= PEER_TOPK]


def _top16_of_128(slabs):
    t = _sort_desc(slabs)
    sub = lax.broadcasted_iota(jnp.int32, t[0].shape, 0)
    for bit in (1, 2, 4):
        if bit == 4:
            partner = [pltpu.roll(v, 4, 0) for v in t]
        else:
            low = (sub & bit) == 0
            partner = [jnp.where(low, pltpu.roll(v, 8 - bit, 0), pltpu.roll(v, bit, 0)) for v in t]
        t = _merge_desc([jnp.maximum(t[i], partner[PEER_TOPK - 1 - i]) for i in range(PEER_TOPK)])
    return t


def _kth_candidate_sums(t1, t2):
    c = [t1[i] + t2[j] for (i, j) in _PEER_CAND]
    c += [jnp.full(c[0].shape, NEG_BIG, F32)] * (64 - len(c))
    return _sort_desc(c)


def _topk_kernel(st_ref, u1_ref, u2_ref, th_ref, t1_s, t2_s):
    nslab = PEER_N_KEYS // 8

    def pack(t, t_s, h):
        for i in range(PEER_TOPK):
            t_s[i, h:h + 1, :] = t[i][0:1, :]

    for h in range(PEER_HEADS):
        u1_ref[h] = st_ref[2 * h] * LOG2E
        u2_ref[h] = st_ref[2 * h + 1] * LOG2E
        pack(_top16_of_128([u1_ref[h, 8 * i:8 * i + 8, :] for i in range(nslab)]), t1_s, h)
        pack(_top16_of_128([u2_ref[h, 8 * i:8 * i + 8, :] for i in range(nslab)]), t2_s, h)
    t1 = [t1_s[i] for i in range(PEER_TOPK)]
    c = _kth_candidate_sums(t1, [t2_s[i] for i in range(PEER_TOPK)])
    z = jnp.exp2(c[0] - c[0])
    for r in range(1, PEER_TOPK):
        z = z + jnp.exp2(c[r] - c[0])
    k2 = c[0] + jnp.log2(z)
    for h in range(PEER_HEADS):
        u2_ref[h] = u2_ref[h] - k2[h:h + 1, :]
    c = _kth_candidate_sums(t1, [t2_s[i] - k2 for i in range(PEER_TOPK)])
    th_ref[...] = c[PEER_TOPK - 1]


def peer_topk(st):
    tm = 128
    hp, nk, n = st.shape
    uspec = pl.BlockSpec((PEER_HEADS, nk, tm), lambda i: (0, 0, i))
    ushape = jax.ShapeDtypeStruct((PEER_HEADS, nk, n), F32)
    return pl.pallas_call(
        _topk_kernel,
        out_shape=[ushape, ushape, jax.ShapeDtypeStruct((PEER_HEADS, n), F32)],
        grid=(n // tm,),
        in_specs=[pl.BlockSpec((hp, nk, tm), lambda i: (0, 0, i))],
        out_specs=[uspec, uspec, pl.BlockSpec((PEER_HEADS, tm), lambda i: (0, i))],
        scratch_shapes=[pltpu.VMEM((PEER_TOPK, 8, 128), F32), pltpu.VMEM((PEER_TOPK, 8, 128), F32)],
        compiler_params=_cparams(("arbitrary",)),
        name="peer_topk",
    )(st)


def _gelu(x):
    return 0.5 * x * (1.0 + lax.erf(x * (2.0 ** -0.5)))


def _peer_kernel(u1_ref, u2_ref, th_ref, h2_ref, u_ref, vt_ref, x1_ref, g2_ref, gf_ref, o_ref,
                 acc_ref, at_ref, p_ref, *, tm, te):
    e = pl.program_id(1)

    @pl.when(e == 0)
    def _():
        acc_ref[...] = jnp.zeros(acc_ref.shape, F32)

    a_per_blk = te // PEER_N_KEYS
    ce = 1024
    n_chunk = te // ce
    heads = range(PEER_HEADS)

    def activations(c):
        rows = slice(c * ce, (c + 1) * ce)
        at_ref[rows, :] = lax.dot_general(u_ref[rows, :], h2_ref[...], (((1,), (1,)), ((), ())),
                                          preferred_element_type=F32)

    def combine_values(c):
        rows = slice(c * ce, (c + 1) * ce)
        acc_ref[...] += jnp.dot(vt_ref[:, rows], p_ref[rows, :], preferred_element_type=F32)

    def weighted(r8, th8, rows, lanes):
        w = None
        for h in heads:
            c = r8[h] + u2_ref[h, rows.start % PEER_N_KEYS:rows.start % PEER_N_KEYS + 8, lanes]
            e = jnp.where(c >= th8[h], jnp.exp2(c), 0.0)
            w = e if w is None else w + e
        return w * _gelu(at_ref[rows, lanes])

    def routed(c):
        for tg in range(tm // 128):
            lanes = slice(tg * 128, (tg + 1) * 128)
            th8 = [jnp.broadcast_to(th_ref[h:h + 1, lanes], (8, 128)) for h in heads]
            for al in range(c * ce // PEER_N_KEYS, (c + 1) * ce // PEER_N_KEYS):
                r8 = [jnp.broadcast_to(u1_ref[h, al:al + 1, lanes], (8, 128)) for h in heads]
                for i in range(0, PEER_N_KEYS, 16):
                    r0 = al * PEER_N_KEYS + i
                    pair = [weighted(r8, th8, slice(r0 + d, r0 + d + 8), lanes) for d in (0, 8)]
                    p_ref[r0:r0 + 16, lanes] = jnp.concatenate(pair, axis=0).astype(BF16)

    activations(0)
    for c in range(n_chunk):
        if c + 1 < n_chunk:
            activations(c + 1)
        routed(c)
        if c >= 1:
            combine_values(c - 1)
    combine_values(n_chunk - 1)

    @pl.when(e == pl.num_programs(1) - 1)
    def _():
        xo = x1_ref[...] + g2_ref[...] * acc_ref[...].T
        o_ref[...] = xo * lax.rsqrt(jnp.mean(xo * xo, axis=-1, keepdims=True) + NORM_EPS) * gf_ref[...]


def peer_experts(u1, u2, th, h2, u_bf, vt_bf, x1, gate2, g_final, tm, te):
    n, d = x1.shape
    n_exp = u_bf.shape[0]
    mod_rows = gate2.shape[0]
    if mod_rows == 1:
        mod_spec = pl.BlockSpec((1, d), lambda i, e: (0, 0))
    else:
        mod_spec = pl.BlockSpec((tm, d), lambda i, e: (i, 0))
    uspec = pl.BlockSpec((PEER_HEADS, PEER_N_KEYS, tm), lambda i, e: (0, 0, i))
    u1spec = pl.BlockSpec((PEER_HEADS, te // PEER_N_KEYS, tm), lambda i, e: (0, e, i))
    return pl.pallas_call(
        functools.partial(_peer_kernel, tm=tm, te=te),
        out_shape=jax.ShapeDtypeStruct((n, d), F32),
        grid=(n // tm, n_exp // te),
        in_specs=[u1spec, uspec, pl.BlockSpec((PEER_HEADS, tm), lambda i, e: (0, i)),
                  pl.BlockSpec((tm, d), lambda i, e: (i, 0)),
                  pl.BlockSpec((te, d), lambda i, e: (e, 0)),
                  pl.BlockSpec((d, te), lambda i, e: (0, e)),
                  pl.BlockSpec((tm, d), lambda i, e: (i, 0)), mod_spec,
                  pl.BlockSpec((1, d), lambda i, e: (0, 0))],
        out_specs=pl.BlockSpec((tm, d), lambda i, e: (i, 0)),
        scratch_shapes=[pltpu.VMEM((d, tm), F32), pltpu.VMEM((te, tm), F32),
                        pltpu.VMEM((te, tm), BF16)],
        compiler_params=_cparams(("arbitrary", "arbitrary")),
        name="peer_experts",
    )(u1, u2, th, h2, u_bf, vt_bf, x1, gate2, g_final)


def _fox_prompt(fq, fk, fv, logf_t, t_blk):
    t = fq.shape[0]
    nh, dh = FOX_HEADS, FOX_HEAD_DIM
    f2, hi, mid, lo = forget_cumsum(logf_t, t_blk)

    def heads(a):
        return a.astype(BF16).reshape(t, nh, dh).transpose(1, 0, 2)

    def heads_t(a):
        return a.astype(BF16).reshape(t, nh, dh).transpose(1, 2, 0)

    qa = jnp.concatenate([heads_t(fq * (dh ** -0.5 * LOG2E)), jnp.full((nh, 3, t), -1.0, BF16),
                          jnp.zeros((nh, 128 - dh - 3, t), BF16)], axis=1)
    ka = jnp.concatenate([heads(fk), jnp.stack([hi, mid, lo], axis=-1).astype(BF16),
                          jnp.full((nh, t, 3), -1.0, BF16),
                          jnp.zeros((nh, t, 128 - dh - 6), BF16)], axis=-1)
    vt = heads_t(fv)
    o_t = fox_prompt_attention(qa, ka, vt, f2[:, ::t_blk].reshape(-1), t_blk)
    return o_t.reshape(nh * dh, t).T


def _prep_weights(w_in, b_fox_f, b_mlstm_i, b_mlstm_f):
    fw, mw = FOX_WIDTH, MLSTM_WIDTH
    o = 3 * fw + FOX_HEADS
    g0 = o + 3 * mw
    wm = jnp.concatenate([w_in[:, :3 * fw], w_in[:, o:o + 3 * mw], w_in[:, g0 + 2 * MLSTM_HEADS:]],
                         axis=1).astype(BF16)
    wg = jnp.concatenate([w_in[:, 3 * fw:o], w_in[:, g0:g0 + 2 * MLSTM_HEADS]], axis=1)
    wg = jnp.pad(wg, ((0, 0), (0, GATE_LANES - wg.shape[1])))
    bg = jnp.concatenate([b_fox_f, b_mlstm_i, b_mlstm_f])
    bg = jnp.pad(bg, (0, GATE_LANES - bg.shape[0])).reshape(1, GATE_LANES)
    return wm, wg, bg


def kernel(x_prompt, x_sample, c_prompt, c_sample, cache_fox_k, cache_fox_v, cache_fox_logf, state_mlstm_C, state_mlstm_n, state_mlstm_m, page_table, w_ada, b_ada, g_norm_mix, g_norm_ffn, w_in, b_fox_f, b_mlstm_i, b_mlstm_f, g_fox_out, g_mlstm_out, w_out, w_peer_q, peer_keys, peer_u, peer_v, g_final):
    D = D_MODEL
    T = x_prompt.shape[1]
    B, S = x_sample.shape[0], x_sample.shape[1]
    n_pages = page_table.shape[1]
    H, DH = FOX_HEADS, FOX_HEAD_DIM
    MH, DK = MLSTM_HEADS, MLSTM_HEAD_DIM
    SP = 8
    SEG_PER_BLK = 128 // SP

    wm, wg, bg = _prep_weights(w_in[0], b_fox_f[0], b_mlstm_i[0], b_mlstm_f[0])
    w_out_b = w_out[0].astype(BF16)
    w_pq_b = w_peer_q[0].astype(BF16)
    keys_b = peer_keys[0].reshape(2 * PEER_HEADS, PEER_N_KEYS, PEER_HALF).astype(BF16)
    u_b = peer_u[0].astype(BF16)
    vt_b = peer_v[0].T.astype(BF16)
    lane = jnp.arange(FOX_WIDTH)
    bd64 = jnp.where(lane[:, None] // DH == lane[None, :] // DH, 1.0 / DH, 0.0).astype(BF16)
    bd128 = jnp.where(lane[:, None] // DK == lane[None, :] // DK, 1.0 / DK, 0.0).astype(BF16)
    g_mix = g_norm_mix[0].reshape(1, D)
    g_ffn = g_norm_ffn[0].reshape(1, D)
    g_fox = g_fox_out[0].reshape(1, FOX_WIDTH)
    g_ml = g_mlstm_out[0].reshape(1, MLSTM_WIDTH)
    g_fin = g_final.reshape(1, D)

    c_all = jnp.concatenate([c_prompt, c_sample], axis=0)
    c_all = jnp.pad(c_all, ((0, (-c_all.shape[0]) % 8), (0, 0)))
    ada = adaln_terms(c_all, w_ada[0], b_ada[0])

    def terms(a):
        return [a[:, i * D:(i + 1) * D] for i in range(6)]

    shift1_p, scale1_p, gate1_p, shift2_p, scale2_p, gate2_p = terms(ada[0:1])
    shift1_s, scale1_s, gate1_s, shift2_s, scale2_s, gate2_s = terms(
        jnp.repeat(ada[1:1 + B], S, axis=0))

    xp = x_prompt.reshape(T, D)
    fq, fk, fv, mq, mk, mv, mo, gt, gtt = in_projection(xp, scale1_p, shift1_p, g_mix, wm, wg, bg, 512)
    fox_o = _fox_prompt(fq, fk, fv, gtt[0:H], 512)
    m_h, c_p, n_p, m_p = mlstm(
        mq.reshape(1, T, MLSTM_WIDTH), mk.reshape(1, T, MLSTM_WIDTH), mv.reshape(1, T, MLSTM_WIDTH),
        gt.reshape(1, T, GATE_LANES), gtt[0:16].reshape(1, 16, T),
        jnp.zeros((1, MH, DK, DK), F32), jnp.zeros((1, MH, 1, DK), F32), jnp.zeros((1, MH, 1, 1), F32),
        256, 256)
    x1, h2, st = combine(fox_o, m_h.reshape(T, MLSTM_WIDTH), mo, xp, gate1_p, scale2_p, shift2_p,
                         g_fox, g_ml, g_ffn, w_out_b, w_pq_b, keys_b, bd64, bd128, 512)
    u1, u2, th = peer_topk(st)
    y_p = peer_experts(u1, u2, th, h2, u_b, vt_b, x1, gate2_p, g_fin, 512, 2048)

    ns = B * S
    xs = x_sample.reshape(ns, D)
    sfq, sfk, sfv, smq, smk, smv, smo, sgt, _ = in_projection(
        xs, scale1_s, shift1_s, g_mix, wm, wg, bg, 256)
    def pad_tok(a):
        a = a.reshape(B, S, a.shape[-1])
        return jnp.pad(a, ((0, 0), (0, SP - S), (0, 0)))

    q4 = (sfq * DH ** -0.5).reshape(B, S, H, 1, DH)
    qbd = (q4 * jnp.eye(H, dtype=F32)[None, None, :, :, None]).astype(BF16).reshape(B, S * H, FOX_WIDTH)
    lf_new_t = jnp.pad(sgt[:, 0:H].reshape(B, S, H).transpose(0, 2, 1),
                       ((0, 0), (0, 0), (0, PAGE_SIZE - S)))
    n_pool = cache_fox_k.shape[1]
    lf_pages = rows_cumsum(cache_fox_logf.transpose(0, 1, 3, 2).reshape(n_pool * H, PAGE_SIZE), 8192)
    so = fox_sample_attention(
        page_table, qbd, pad_tok(sfk), pad_tok(sfv), lf_new_t,
        cache_fox_k.transpose(0, 1, 3, 4, 2), cache_fox_v.transpose(0, 1, 3, 4, 2),
        lf_pages.reshape(1, n_pool, H, PAGE_SIZE), 16)
    pad_gate = jnp.zeros((GATE_LANES,), F32).at[H:H + MH].set(NEG_BIG)
    sg = jnp.concatenate([sgt.reshape(B, S, GATE_LANES),
                          jnp.broadcast_to(pad_gate, (B, SP - S, GATE_LANES))], axis=1)
    sg = sg.reshape(B // SEG_PER_BLK, 128, GATE_LANES)

    def blk(a):
        return pad_tok(a).reshape(B // SEG_PER_BLK, 128, a.shape[-1])

    sm_h, c_s, n_s, m_s = mlstm(
        blk(smq), blk(smk), blk(smv), sg, sg.transpose(0, 2, 1)[:, 0:16],
        state_mlstm_C[0], state_mlstm_n[0].reshape(B, MH, 1, DK), state_mlstm_m[0].reshape(B, MH, 1, 1),
        128, SP)
    sm_h = sm_h.reshape(B, SP, MLSTM_WIDTH)[:, 0:S].reshape(ns, MLSTM_WIDTH)
    sx1, sh2, sst = combine(so.reshape(ns, FOX_WIDTH), sm_h, smo, xs, gate1_s, scale2_s, shift2_s,
                            g_fox, g_ml, g_ffn, w_out_b, w_pq_b, keys_b, bd64, bd128, 256)
    su1, su2, sth = peer_topk(sst)
    y_s = peer_experts(su1, su2, sth, sh2, u_b, vt_b, sx1, gate2_s, g_fin, 512, 1024)

    return (y_p.reshape(1, T, D), y_s.reshape(B, S, D),
            fk.reshape(1, 1, T, H, DH), fv.reshape(1, 1, T, H, DH), gt[:, 0:H].reshape(1, 1, T, H),
            c_p.reshape(1, 1, MH, DK, DK), n_p.reshape(1, 1, MH, DK), m_p.reshape(1, 1, MH),
            sfk.reshape(1, B, S, H, DH), sfv.reshape(1, B, S, H, DH), sgt[:, 0:H].reshape(1, B, S, H),
            c_s.reshape(1, B, MH, DK, DK), n_s.reshape(1, B, MH, DK), m_s.reshape(1, B, MH))
```

```python
import functools
import math

import jax
import jax.numpy as jnp
from jax import lax
from jax.experimental import pallas as pl
from jax.experimental.pallas import tpu as pltpu

F32 = jnp.float32
BF16 = jnp.bfloat16
HIGHEST = lax.Precision.HIGHEST

D_MODEL = 1024
FOX_HEADS = 8
FOX_HEAD_DIM = 64
FOX_WIDTH = FOX_HEADS * FOX_HEAD_DIM
MLSTM_HEADS = 4
MLSTM_HEAD_DIM = 128
MLSTM_WIDTH = MLSTM_HEADS * MLSTM_HEAD_DIM
PAGE_SIZE = 128
PEER_HEADS = 8
PEER_N_KEYS = 128
PEER_TOPK = 16
PEER_HALF = 128
NORM_EPS = 1e-6
GATE_LANES = 128
NEG_BIG = -1e30
LOG2E = 1.4426950408889634
VMEM_LIMIT = 56 * 1024 * 1024


def _cparams(sem, flags=None):
    return pltpu.CompilerParams(dimension_semantics=sem, vmem_limit_bytes=VMEM_LIMIT, flags=flags)


def _log_sigmoid(x):
    return jnp.minimum(x, 0.0) - jnp.log1p(jnp.exp(-jnp.abs(x)))


def _adaln_kernel(c_ref, w_ref, b_ref, o_ref):
    c = c_ref[...]
    s = c / (1.0 + jnp.exp(-c))
    o_ref[...] = jnp.dot(s, w_ref[...], precision=HIGHEST, preferred_element_type=F32) + b_ref[...]


def adaln_terms(c, w_ada, b_ada):
    rows, d = c.shape
    n = w_ada.shape[1]
    tn = 768
    return pl.pallas_call(
        _adaln_kernel,
        out_shape=jax.ShapeDtypeStruct((rows, n), F32),
        grid=(n // tn,),
        in_specs=[pl.BlockSpec((rows, d), lambda j: (0, 0)),
                  pl.BlockSpec((d, tn), lambda j: (0, j)),
                  pl.BlockSpec((1, tn), lambda j: (0, j))],
        out_specs=pl.BlockSpec((rows, tn), lambda j: (0, j)),
        compiler_params=_cparams(("arbitrary",)),
        name="adaln",
    )(c, w_ada, b_ada.reshape(1, n))


def _inproj_kernel(x_ref, sc_ref, sh_ref, g_ref, wm_ref, wg_ref, bg_ref,
                   fq_ref, fk_ref, fv_ref, mq_ref, mk_ref, mv_ref, mo_ref, gt_ref, gtt_ref):
    x = x_ref[...]
    y = x * lax.rsqrt(jnp.mean(x * x, axis=-1, keepdims=True) + NORM_EPS) * g_ref[...]
    h = y * (1.0 + sc_ref[...]) + sh_ref[...]
    hb = h.astype(BF16)
    for i, o_ref in enumerate((fq_ref, fk_ref, fv_ref, mq_ref, mk_ref, mv_ref, mo_ref)):
        o_ref[...] = jnp.dot(hb, wm_ref[:, i * 512:(i + 1) * 512], preferred_element_type=F32)
    zg = jnp.dot(h, wg_ref[...], precision=HIGHEST, preferred_element_type=F32) + bg_ref[...]
    col = lax.broadcasted_iota(jnp.int32, zg.shape, 1)
    is_i = (col >= FOX_HEADS) & (col < FOX_HEADS + MLSTM_HEADS)
    gt = jnp.where(is_i, zg, _log_sigmoid(zg))
    gt_ref[...] = gt
    gtt_ref[...] = gt.T


def in_projection(x, scale, shift, g, wm, wg, bg, tm):
    n, d = x.shape
    mod_rows = scale.shape[0]
    if mod_rows == 1:
        mod_spec = pl.BlockSpec((1, d), lambda i: (0, 0))
    else:
        mod_spec = pl.BlockSpec((tm, d), lambda i: (i, 0))
    o512 = jax.ShapeDtypeStruct((n, 512), F32)
    s512 = pl.BlockSpec((tm, 512), lambda i: (i, 0))
    return pl.pallas_call(
        _inproj_kernel,
        out_shape=[o512] * 7 + [jax.ShapeDtypeStruct((n, GATE_LANES), F32),
                                jax.ShapeDtypeStruct((GATE_LANES, n), F32)],
        grid=(n // tm,),
        in_specs=[pl.BlockSpec((tm, d), lambda i: (i, 0)), mod_spec, mod_spec,
                  pl.BlockSpec((1, d), lambda i: (0, 0)),
                  pl.BlockSpec(wm.shape, lambda i: (0, 0)),
                  pl.BlockSpec(wg.shape, lambda i: (0, 0)),
                  pl.BlockSpec((1, GATE_LANES), lambda i: (0, 0))],
        out_specs=[s512] * 7 + [pl.BlockSpec((tm, GATE_LANES), lambda i: (i, 0)),
                                pl.BlockSpec((GATE_LANES, tm), lambda i: (0, i))],
        compiler_params=_cparams(("arbitrary",)),
        name="inproj",
    )(x, scale, shift, g, wm, wg, bg)


def _lane_cumsum(x, stride=1):
    n = x.shape[-1]
    lane = lax.broadcasted_iota(jnp.int32, x.shape, x.ndim - 1)
    sh = stride
    while sh < n:
        x = x + jnp.where(lane >= sh, pltpu.roll(x, sh, x.ndim - 1), 0.0)
        sh *= 2
    return x


def _cumsum_kernel(x_ref, f_ref, hi_ref, mid_ref, lo_ref, *, blk):
    f = _lane_cumsum(x_ref[...]) * LOG2E
    f_ref[...] = f
    for b in range(f.shape[-1] // blk):
        cols = slice(b * blk, (b + 1) * blk)
        rel = f[:, cols] - f[:, b * blk:b * blk + 1]
        hi = rel.astype(BF16).astype(F32)
        mid = (rel - hi).astype(BF16).astype(F32)
        hi_ref[:, cols] = hi
        mid_ref[:, cols] = mid
        lo_ref[:, cols] = (rel - hi - mid).astype(BF16).astype(F32)


def forget_cumsum(x, blk):
    out = jax.ShapeDtypeStruct(x.shape, F32)
    return pl.pallas_call(
        functools.partial(_cumsum_kernel, blk=blk),
        out_shape=[out, out, out, out],
        name="cumsum",
    )(x)


def _rows_cumsum_kernel(x_ref, o_ref):
    o_ref[...] = _lane_cumsum(x_ref[...])


def rows_cumsum(x, tr):
    rows, n = x.shape
    return pl.pallas_call(
        _rows_cumsum_kernel,
        out_shape=jax.ShapeDtypeStruct(x.shape, F32),
        grid=(rows // tr,),
        in_specs=[pl.BlockSpec((tr, n), lambda i: (i, 0))],
        out_specs=pl.BlockSpec((tr, n), lambda i: (i, 0)),
        compiler_params=_cparams(("arbitrary",)),
        name="page_cumsum",
    )(x)


def _flash_kernel(qi_ref, ki_ref, fs_ref, qa_ref, ka_ref, vt_ref, o_ref, m_ref, l_ref, acc_ref, *, nb):
    s_idx = pl.program_id(0)
    qi = qi_ref[s_idx]
    ki = ki_ref[s_idx]
    nh, dh, tq = qa_ref.shape
    da = ka_ref.shape[2]

    @pl.when(ki == 0)
    def _():
        m_ref[...] = jnp.full(m_ref.shape, -jnp.inf, F32)
        l_ref[...] = jnp.zeros(l_ref.shape, F32)
        acc_ref[...] = jnp.zeros(acc_ref.shape, F32)

    def mask(x):
        key = lax.broadcasted_iota(jnp.int32, x.shape, 0)
        qry = lax.broadcasted_iota(jnp.int32, x.shape, 1)
        return jnp.where(key <= qry, x, -jnp.inf)

    row = lax.broadcasted_iota(jnp.int32, (16, tq), 0)
    fill = jnp.zeros((da - dh - 16, tq), BF16)

    def query_operand(h, aug):
        return jnp.concatenate([qa_ref[h], aug.astype(BF16), fill], axis=0)

    def new_max(h, masked):
        raw = jnp.dot(ka_ref[h], query_operand(h, jnp.where(row < 3, -1.0, 0.0)),
                      preferred_element_type=F32)
        if masked:
            raw = mask(raw)
        cb = -fs_ref[h * nb + ki]
        m_prev = m_ref[h]
        m_new = jnp.maximum(m_prev, jnp.max(raw, axis=0, keepdims=True) + cb)
        m_ref[h] = m_new
        x = m_new - cb
        hi = x.astype(BF16).astype(F32)
        mid = (x - hi).astype(BF16).astype(F32)
        lo = (x - hi - mid).astype(BF16).astype(F32)
        aug = jnp.where(row < 3, -1.0,
                        jnp.where(row == 3, hi, jnp.where(row == 4, mid, jnp.where(row == 5, lo, 0.0))))
        return query_operand(h, aug), jnp.exp2(m_prev - m_new)

    def probabilities(h, q2, alpha, masked):
        raw = jnp.dot(ka_ref[h], q2, preferred_element_type=F32)
        if masked:
            raw = mask(raw)
        p = jnp.exp2(raw)
        l_ref[h] = alpha * l_ref[h] + jnp.sum(p, axis=0, keepdims=True)
        return p.astype(BF16)

    def accumulate(h, p, alpha):
        acc_ref[h] = alpha * acc_ref[h] + jnp.dot(vt_ref[h], p, preferred_element_type=F32)

    def step(masked):
        nxt = new_max(0, masked)
        pending = None
        for h in range(nh):
            q2, alpha = nxt
            if h + 1 < nh:
                nxt = new_max(h + 1, masked)
            p = probabilities(h, q2, alpha, masked)
            if pending is not None:
                accumulate(*pending)
            pending = (h, p, alpha)
        accumulate(*pending)

    @pl.when(ki < qi)
    def _():
        step(False)

    @pl.when(ki == qi)
    def _():
        step(True)
        o_ref[...] = acc_ref[...] / l_ref[...]


def fox_prompt_attention(qa, ka, vt, f_start, t_blk):
    h, t, da = ka.shape
    dh = vt.shape[1]
    nb = t // t_blk
    pairs = [(i, j) for i in range(nb) for j in range(i + 1)]
    qi_arr = jnp.array([p[0] for p in pairs], jnp.int32)
    ki_arr = jnp.array([p[1] for p in pairs], jnp.int32)
    grid_spec = pltpu.PrefetchScalarGridSpec(
        num_scalar_prefetch=3,
        grid=(len(pairs),),
        in_specs=[pl.BlockSpec((h, dh, t_blk), lambda s, qi, ki, fs: (0, 0, qi[s])),
                  pl.BlockSpec((h, t_blk, da), lambda s, qi, ki, fs: (0, ki[s], 0)),
                  pl.BlockSpec((h, dh, t_blk), lambda s, qi, ki, fs: (0, 0, ki[s]))],
        out_specs=pl.BlockSpec((h, dh, t_blk), lambda s, qi, ki, fs: (0, 0, qi[s])),
        scratch_shapes=[pltpu.VMEM((h, 1, t_blk), F32), pltpu.VMEM((h, 1, t_blk), F32),
                        pltpu.VMEM((h, dh, t_blk), F32)],
    )
    return pl.pallas_call(
        functools.partial(_flash_kernel, nb=nb),
        out_shape=jax.ShapeDtypeStruct((h, dh, t), F32),
        grid_spec=grid_spec,
        compiler_params=_cparams(("arbitrary",)),
        name="fox_prompt",
    )(qi_arr, ki_arr, f_start, qa, ka, vt)


def _mlstm_kernel(q_ref, k_ref, v_ref, g_ref, gt_ref, c0_ref, n0_ref, m0_ref,
                  h_ref, c_out_ref, n_out_ref, m_out_ref, c_s, n_s, m_s, *, rows, seg_len):
    nseg = rows // seg_len
    dk = MLSTM_HEAD_DIM
    c_idx = pl.program_id(1)

    @pl.when(c_idx == 0)
    def _():
        c_s[...] = c0_ref[...]
        n_s[...] = n0_ref[...]
        m_s[...] = m0_ref[...]

    g = g_ref[0]
    gt = gt_ref[0]
    row = lax.broadcasted_iota(jnp.int32, (rows, rows), 0)
    col = lax.broadcasted_iota(jnp.int32, (rows, rows), 1)
    if nseg == 1:
        causal = col <= row
        anti = row <= col
    else:
        same = (row // seg_len) == (col // seg_len)
        causal = (col <= row) & same
        anti = (row <= col) & same
    rid = lax.broadcasted_iota(jnp.int32, (rows, 1), 0) // seg_len

    def per_row(vals):
        out = jnp.broadcast_to(vals[0], (rows, 1))
        for j in range(1, nseg):
            out = jnp.where(rid == j, vals[j], out)
        return out

    for h in range(MLSTM_HEADS):
        lane = slice(h * dk, (h + 1) * dk)
        q = q_ref[0, :, lane]
        k = k_ref[0, :, lane] * (dk ** -0.5)
        v = v_ref[0, :, lane]
        i_row = gt[8 + h:9 + h, :]
        lf_row = gt[12 + h:13 + h, :]
        i_col = g[:, 8 + h:9 + h]
        lf_col = g[:, 12 + h:13 + h]
        b_col = jnp.sum(jnp.where(causal, lf_row, 0.0), axis=1, keepdims=True)
        b_row = jnp.sum(jnp.where(anti, lf_col, 0.0), axis=0, keepdims=True)
        log_d = jnp.where(causal, b_col - b_row + i_row, -jnp.inf)
        m_prev = [m_s[j, h] for j in range(nseg)]
        m_inter = b_col + per_row(m_prev)
        m_t = jnp.maximum(m_inter, jnp.max(log_d, axis=1, keepdims=True))
        d_mat = jnp.exp(log_d - m_t)
        inter = jnp.exp(m_inter - m_t)
        qb = q.astype(BF16)
        s_mat = lax.dot_general(qb, k.astype(BF16), (((1,), (1,)), ((), ())),
                                preferred_element_type=F32) * d_mat
        num = jnp.dot(s_mat.astype(BF16), v.astype(BF16), preferred_element_type=F32)
        qc = jnp.concatenate(
            [jnp.dot(qb[j * seg_len:(j + 1) * seg_len], c_s[j, h].astype(BF16),
                     preferred_element_type=F32) for j in range(nseg)], axis=0)
        n_rows = jnp.concatenate(
            [jnp.broadcast_to(n_s[j, h], (seg_len, dk)) for j in range(nseg)], axis=0)
        qn = jnp.sum(q * n_rows, axis=1, keepdims=True)
        num = inter * qc + num
        den = inter * qn + jnp.sum(s_mat, axis=1, keepdims=True)
        h_ref[0, :, lane] = num / jnp.maximum(jnp.abs(den), jnp.exp(-m_t))

        last = [(j + 1) * seg_len - 1 for j in range(nseg)]
        b_last = [b_col[r:r + 1, :] for r in last]
        m_new = [m_t[r:r + 1, :] for r in last]
        w_end = jnp.exp(per_row(b_last) - b_col + i_col - per_row(m_new))
        kw = k * w_end
        kwt = kw.T.astype(BF16)
        vb = v.astype(BF16)
        if nseg > 1:
            vb = jnp.concatenate([jnp.where(rid == j, vb, jnp.zeros_like(vb))
                                  for j in range(nseg)], axis=1)
        upd = jnp.dot(kwt, vb, preferred_element_type=F32)
        for j in range(nseg):
            decay = jnp.exp(b_last[j] + m_prev[j] - m_new[j])
            kw_j = kw if nseg == 1 else jnp.where(rid == j, kw, 0.0)
            c_s[j, h] = decay * c_s[j, h] + upd[:, j * dk:(j + 1) * dk]
            n_s[j, h] = decay * n_s[j, h] + jnp.sum(kw_j, axis=0, keepdims=True)
            m_s[j, h] = m_new[j]

    @pl.when(c_idx == pl.num_programs(1) - 1)
    def _():
        c_out_ref[...] = c_s[...]
        n_out_ref[...] = n_s[...]
        m_out_ref[...] = m_s[...]


def mlstm(q, k, v, gates, gates_t, c0, n0, m0, rows, seg_len):
    G, tg, w = q.shape
    nseg = rows // seg_len
    nc = tg // rows
    assert nseg == 1 or nc == 1
    qspec = pl.BlockSpec((1, rows, w), lambda gi, c: (gi, c, 0))
    cspec = pl.BlockSpec((nseg, MLSTM_HEADS, 128, 128), lambda gi, c: (gi, 0, 0, 0))
    nspec = pl.BlockSpec((nseg, MLSTM_HEADS, 1, 128), lambda gi, c: (gi, 0, 0, 0))
    mspec = pl.BlockSpec((nseg, MLSTM_HEADS, 1, 1), lambda gi, c: (gi, 0, 0, 0))
    return pl.pallas_call(
        functools.partial(_mlstm_kernel, rows=rows, seg_len=seg_len),
        out_shape=[jax.ShapeDtypeStruct((G, tg, w), F32),
                   jax.ShapeDtypeStruct(c0.shape, F32),
                   jax.ShapeDtypeStruct(n0.shape, F32),
                   jax.ShapeDtypeStruct(m0.shape, F32)],
        grid=(G, nc),
        in_specs=[qspec, qspec, qspec,
                  pl.BlockSpec((1, rows, GATE_LANES), lambda gi, c: (gi, c, 0)),
                  pl.BlockSpec((1, 16, rows), lambda gi, c: (gi, 0, c)),
                  cspec, nspec, mspec],
        out_specs=[qspec, cspec, nspec, mspec],
        scratch_shapes=[pltpu.VMEM((nseg, MLSTM_HEADS, 128, 128), F32),
                        pltpu.VMEM((nseg, MLSTM_HEADS, 1, 128), F32),
                        pltpu.VMEM((nseg, MLSTM_HEADS, 1, 1), F32)],
        compiler_params=_cparams(("arbitrary", "arbitrary")),
        name="mlstm",
    )(q, k, v, gates, gates_t, c0, n0, m0)


def _fox_sample_kernel(pt_ref, qbd_ref, knew_ref, vnew_ref, lfnew_ref, *rest, pps):
    k_refs = rest[0:pps]
    v_refs = rest[pps:2 * pps]
    lf_refs = rest[2 * pps:3 * pps]
    o_ref = rest[3 * pps]
    m_ref, l_ref, acc_ref, f_ref = rest[3 * pps + 1:]
    step = pl.program_id(1)
    nq, w = qbd_ref.shape[1], qbd_ref.shape[2]
    reps = nq // FOX_HEADS

    @pl.when(step == 0)
    def _():
        m_ref[...] = jnp.full(m_ref.shape, -jnp.inf, F32)
        l_ref[...] = jnp.zeros(l_ref.shape, F32)
        acc_ref[...] = jnp.zeros(acc_ref.shape, F32)
        f_ref[...] = jnp.zeros(f_ref.shape, F32)

    qbd = qbd_ref[0]

    def update(logits, pv_fn):
        m_prev = m_ref[...]
        m_new = jnp.maximum(m_prev, jnp.max(logits, axis=-1, keepdims=True))
        alpha = jnp.exp(m_prev - m_new)
        p = jnp.exp(logits - m_new)
        l_ref[...] = alpha * l_ref[...] + jnp.sum(p, axis=-1, keepdims=True)
        acc_ref[...] = alpha * acc_ref[...] + pv_fn(p.astype(BF16))
        m_ref[...] = m_new

    carry = f_ref[...]
    f_pages = []
    for i in range(pps):
        f_page = lf_refs[i][0, 0] + carry
        carry = f_page[:, PAGE_SIZE - 1:PAGE_SIZE]
        f_pages.append(jnp.concatenate([f_page] * reps, axis=0))
    f_ref[...] = carry
    k_all = jnp.concatenate([r[0, 0].reshape(w, PAGE_SIZE).astype(BF16) for r in k_refs], axis=1)
    s = jnp.dot(qbd, k_all, preferred_element_type=F32)

    def pv_pages(pb):
        v_all = jnp.concatenate([r[0, 0].reshape(w, PAGE_SIZE).astype(BF16) for r in v_refs], axis=1)
        return lax.dot_general(pb, v_all, (((1,), (1,)), ((), ())), preferred_element_type=F32)

    update(s - jnp.concatenate(f_pages, axis=1), pv_pages)

    @pl.when(step == pl.num_programs(1) - 1)
    def _():
        pad = jnp.zeros((PAGE_SIZE - knew_ref.shape[1], w), F32)
        k_new = jnp.concatenate([knew_ref[0], pad], axis=0).astype(BF16)
        v_new = jnp.concatenate([vnew_ref[0], pad], axis=0).astype(BF16)
        f_new = _lane_cumsum(lfnew_ref[0]) + carry
        lg = lax.dot_general(qbd, k_new, (((1,), (1,)), ((), ())), preferred_element_type=F32)
        lg = lg - jnp.concatenate([f_new] * reps, axis=0)
        row = lax.broadcasted_iota(jnp.int32, lg.shape, 0)
        col = lax.broadcasted_iota(jnp.int32, lg.shape, 1)
        lg = jnp.where(col <= row // FOX_HEADS, lg, -jnp.inf)
        update(lg, lambda pb: jnp.dot(pb, v_new, preferred_element_type=F32))
        o = acc_ref[...] / l_ref[...]
        row = lax.broadcasted_iota(jnp.int32, o.shape, 0)
        col = lax.broadcasted_iota(jnp.int32, o.shape, 1)
        o = jnp.where(col // FOX_HEAD_DIM == row % FOX_HEADS, o, 0.0)
        o_ref[0] = jnp.concatenate(
            [jnp.sum(o[j * FOX_HEADS:(j + 1) * FOX_HEADS], axis=0, keepdims=True)
             for j in range(reps)], axis=0)


def fox_sample_attention(page_table, qbd, k_new, v_new, lf_new_t, k_cache_t, v_cache_t, lf_cache_t, pps):
    b, nq, w = qbd.shape
    n_pages = page_table.shape[1]
    pt = page_table.reshape(-1)
    s_new = nq // FOX_HEADS

    def page_map(i, nd):
        return lambda bi, s, pt_ref: (0, pt_ref[bi * n_pages + s * pps + i]) + (0,) * nd

    def bmap(bi, s, pt_ref):
        return (bi, 0, 0)

    kv_blk = (1, 1, FOX_HEADS, FOX_HEAD_DIM, PAGE_SIZE)
    in_specs = [pl.BlockSpec((1, nq, w), bmap),
                pl.BlockSpec((1,) + k_new.shape[1:], bmap),
                pl.BlockSpec((1,) + v_new.shape[1:], bmap),
                pl.BlockSpec((1,) + lf_new_t.shape[1:], bmap)]
    in_specs += [pl.BlockSpec(kv_blk, page_map(i, 3)) for i in range(pps)]
    in_specs += [pl.BlockSpec(kv_blk, page_map(i, 3)) for i in range(pps)]
    in_specs += [pl.BlockSpec((1, 1, FOX_HEADS, PAGE_SIZE), page_map(i, 2)) for i in range(pps)]
    grid_spec = pltpu.PrefetchScalarGridSpec(
        num_scalar_prefetch=1,
        grid=(b, n_pages // pps),
        in_specs=in_specs,
        out_specs=pl.BlockSpec((1, s_new, w), bmap),
        scratch_shapes=[pltpu.VMEM((nq, 1), F32), pltpu.VMEM((nq, 1), F32),
                        pltpu.VMEM((nq, w), F32), pltpu.VMEM((FOX_HEADS, 1), F32)],
    )
    return pl.pallas_call(
        functools.partial(_fox_sample_kernel, pps=pps),
        out_shape=jax.ShapeDtypeStruct((b, s_new, w), F32),
        grid_spec=grid_spec,
        compiler_params=_cparams(("arbitrary", "arbitrary")),
        name="fox_sample",
    )(pt, qbd, k_new, v_new, lf_new_t, *([k_cache_t] * pps), *([v_cache_t] * pps),
      *([lf_cache_t] * pps))


def _group_mean_sq(x, bd):
    sq = x * x
    hi = sq.astype(BF16)
    lo = (sq - hi.astype(F32)).astype(BF16)
    return (jnp.dot(hi, bd, preferred_element_type=F32) + jnp.dot(lo, bd, preferred_element_type=F32))


def _combine_kernel(fo_ref, mh_ref, mo_ref, x_ref, g1_ref, sc2_ref, sh2_ref, gfo_ref, gml_ref,
                    gffn_ref, wout_ref, wpq_ref, keys_ref, bd64_ref, bd128_ref,
                    x1_ref, h2_ref, st_ref):
    fo = fo_ref[...]
    a_out = fo * lax.rsqrt(_group_mean_sq(fo, bd64_ref[...]) + NORM_EPS) * gfo_ref[...]
    mh = mh_ref[...]
    mo = mo_ref[...]
    b_out = (mh * lax.rsqrt(_group_mean_sq(mh, bd128_ref[...]) + NORM_EPS) * gml_ref[...]
             / (1.0 + jnp.exp(-mo)))
    cat = jnp.concatenate([a_out, b_out], axis=1).astype(BF16)
    x1 = x_ref[...] + g1_ref[...] * jnp.dot(cat, wout_ref[...], preferred_element_type=F32)
    x1_ref[...] = x1
    y = x1 * lax.rsqrt(jnp.mean(x1 * x1, axis=-1, keepdims=True) + NORM_EPS) * gffn_ref[...]
    h2 = (y * (1.0 + sc2_ref[...]) + sh2_ref[...]).astype(BF16)
    h2_ref[...] = h2
    pq = jnp.dot(h2, wpq_ref[...], preferred_element_type=F32).astype(BF16)
    for hp in range(2 * PEER_HEADS):
        st_ref[hp] = lax.dot_general(keys_ref[hp], pq[:, hp * PEER_HALF:(hp + 1) * PEER_HALF],
                                     (((1,), (1,)), ((), ())), preferred_element_type=F32)


def combine(fox_o, m_h, mo, x, gate1, scale2, shift2, g_fox, g_ml, g_ffn, w_out, w_pq, keys,
            bd64, bd128, tm):
    n, d = x.shape
    mod_rows = gate1.shape[0]
    if mod_rows == 1:
        mod_spec = pl.BlockSpec((1, d), lambda i: (0, 0))
    else:
        mod_spec = pl.BlockSpec((tm, d), lambda i: (i, 0))

    def full(a):
        return pl.BlockSpec(a.shape, lambda i: (0,) * a.ndim)

    s512 = pl.BlockSpec((tm, 512), lambda i: (i, 0))
    sd = pl.BlockSpec((tm, d), lambda i: (i, 0))
    return pl.pallas_call(
        _combine_kernel,
        out_shape=[jax.ShapeDtypeStruct((n, d), F32), jax.ShapeDtypeStruct((n, d), BF16),
                   jax.ShapeDtypeStruct((2 * PEER_HEADS, PEER_N_KEYS, n), F32)],
        grid=(n // tm,),
        in_specs=[s512, s512, s512, sd, mod_spec, mod_spec, mod_spec, full(g_fox), full(g_ml),
                  full(g_ffn), full(w_out), full(w_pq), full(keys), full(bd64), full(bd128)],
        out_specs=[sd, sd, pl.BlockSpec((2 * PEER_HEADS, PEER_N_KEYS, tm), lambda i: (0, 0, i))],
        compiler_params=_cparams(("arbitrary",)),
        name="combine",
    )(fox_o, m_h, mo, x, gate1, scale2, shift2, g_fox, g_ml, g_ffn, w_out, w_pq, keys, bd64, bd128)


def _sort_desc(a):
    a = list(a)
    n = len(a)
    k = 2
    while k <= n:
        j = k // 2
        while j >= 1:
            for i in range(n):
                p = i ^ j
                if p > i:
                    hi, lo = jnp.maximum(a[i], a[p]), jnp.minimum(a[i], a[p])
                    a[i], a[p] = (hi, lo) if (i & k) == 0 else (lo, hi)
            j //= 2
        k *= 2
    return a


def _merge_desc(a):
    a = list(a)
    j = len(a) // 2
    while j >= 1:
        for i in range(len(a)):
            p = i ^ j
            if p > i:
                a[i], a[p] = jnp.maximum(a[i], a[p]), jnp.minimum(a[i], a[p])
        j //= 2
    return a


_PEER_CAND = [(i, j) for i in range(PEER_TOPK) for j in range(PEER_TOPK)
              if (i + 1) * (j + 1) <= PEER_TOPK]


def _top16_of_128(slabs):
    t = _sort_desc(slabs)
    sub = lax.broadcasted_iota(jnp.int32, t[0].shape, 0)
    for bit in (1, 2, 4):
        if bit == 4:
            partner = [pltpu.roll(v, 4, 0) for v in t]
        else:
            low = (sub & bit) == 0
            partner = [jnp.where(low, pltpu.roll(v, 8 - bit, 0), pltpu.roll(v, bit, 0)) for v in t]
        t = _merge_desc([jnp.maximum(t[i], partner[PEER_TOPK - 1 - i]) for i in range(PEER_TOPK)])
    return t


def _kth_candidate_sums(t1, t2):
    c = [t1[i] + t2[j] for (i, j) in _PEER_CAND]
    c += [jnp.full(c[0].shape, NEG_BIG, F32)] * (64 - len(c))
    return _sort_desc(c)


def _topk_kernel(st_ref, u1_ref, u2_ref, th_ref, t1_s, t2_s):
    nslab = PEER_N_KEYS // 8

    def pack(t, t_s, h):
        for i in range(PEER_TOPK):
            t_s[i, h:h + 1, :] = t[i][0:1, :]

    for h in range(PEER_HEADS):
        u1_ref[h] = st_ref[2 * h] * LOG2E
        u2_ref[h] = st_ref[2 * h + 1] * LOG2E
        pack(_top16_of_128([u1_ref[h, 8 * i:8 * i + 8, :] for i in range(nslab)]), t1_s, h)
        pack(_top16_of_128([u2_ref[h, 8 * i:8 * i + 8, :] for i in range(nslab)]), t2_s, h)
    t1 = [t1_s[i] for i in range(PEER_TOPK)]
    c = _kth_candidate_sums(t1, [t2_s[i] for i in range(PEER_TOPK)])
    z = jnp.exp2(c[0] - c[0])
    for r in range(1, PEER_TOPK):
        z = z + jnp.exp2(c[r] - c[0])
    k2 = c[0] + jnp.log2(z)
    for h in range(PEER_HEADS):
        u2_ref[h] = u2_ref[h] - k2[h:h + 1, :]
    c = _kth_candidate_sums(t1, [t2_s[i] - k2 for i in range(PEER_TOPK)])
    th_ref[...] = c[PEER_TOPK - 1]


def peer_topk(st):
    tm = 128
    hp, nk, n = st.shape
    uspec = pl.BlockSpec((PEER_HEADS, nk, tm), lambda i: (0, 0, i))
    ushape = jax.ShapeDtypeStruct((PEER_HEADS, nk, n), F32)
    return pl.pallas_call(
        _topk_kernel,
        out_shape=[ushape, ushape, jax.ShapeDtypeStruct((PEER_HEADS, n), F32)],
        grid=(n // tm,),
        in_specs=[pl.BlockSpec((hp, nk, tm), lambda i: (0, 0, i))],
        out_specs=[uspec, uspec, pl.BlockSpec((PEER_HEADS, tm), lambda i: (0, i))],
        scratch_shapes=[pltpu.VMEM((PEER_TOPK, 8, 128), F32), pltpu.VMEM((PEER_TOPK, 8, 128), F32)],
        compiler_params=_cparams(("arbitrary",)),
        name="peer_topk",
    )(st)


def _gelu(x):
    return 0.5 * x * (1.0 + lax.erf(x * (2.0 ** -0.5)))


def _peer_kernel(u1_ref, u2_ref, th_ref, h2_ref, u_ref, vt_ref, x1_ref, g2_ref, gf_ref, o_ref,
                 acc_ref, at_ref, p_ref, *, tm, te):
    e = pl.program_id(1)

    @pl.when(e == 0)
    def _():
        acc_ref[...] = jnp.zeros(acc_ref.shape, F32)

    a_per_blk = te // PEER_N_KEYS
    ce = 1024
    n_chunk = te // ce
    heads = range(PEER_HEADS)

    def activations(c):
        rows = slice(c * ce, (c + 1) * ce)
        at_ref[rows, :] = lax.dot_general(u_ref[rows, :], h2_ref[...], (((1,), (1,)), ((), ())),
                                          preferred_element_type=F32)

    def combine_values(c):
        rows = slice(c * ce, (c + 1) * ce)
        acc_ref[...] += jnp.dot(vt_ref[:, rows], p_ref[rows, :], preferred_element_type=F32)

    def weighted(r8, th8, rows, lanes):
        w = None
        for h in heads:
            c = r8[h] + u2_ref[h, rows.start % PEER_N_KEYS:rows.start % PEER_N_KEYS + 8, lanes]
            e = jnp.where(c >= th8[h], jnp.exp2(c), 0.0)
            w = e if w is None else w + e
        return w * _gelu(at_ref[rows, lanes])

    def routed(c):
        for tg in range(tm // 128):
            lanes = slice(tg * 128, (tg + 1) * 128)
            th8 = [jnp.broadcast_to(th_ref[h:h + 1, lanes], (8, 128)) for h in heads]
            for al in range(c * ce // PEER_N_KEYS, (c + 1) * ce // PEER_N_KEYS):
                r8 = [jnp.broadcast_to(u1_ref[h, al:al + 1, lanes], (8, 128)) for h in heads]
                for i in range(0, PEER_N_KEYS, 16):
                    r0 = al * PEER_N_KEYS + i
                    pair = [weighted(r8, th8, slice(r0 + d, r0 + d + 8), lanes) for d in (0, 8)]
                    p_ref[r0:r0 + 16, lanes] = jnp.concatenate(pair, axis=0).astype(BF16)

    activations(0)
    for c in range(n_chunk):
        if c + 1 < n_chunk:
            activations(c + 1)
        routed(c)
        if c >= 1:
            combine_values(c - 1)
    combine_values(n_chunk - 1)

    @pl.when(e == pl.num_programs(1) - 1)
    def _():
        xo = x1_ref[...] + g2_ref[...] * acc_ref[...].T
        o_ref[...] = xo * lax.rsqrt(jnp.mean(xo * xo, axis=-1, keepdims=True) + NORM_EPS) * gf_ref[...]


def peer_experts(u1, u2, th, h2, u_bf, vt_bf, x1, gate2, g_final, tm, te):
    n, d = x1.shape
    n_exp = u_bf.shape[0]
    mod_rows = gate2.shape[0]
    if mod_rows == 1:
        mod_spec = pl.BlockSpec((1, d), lambda i, e: (0, 0))
    else:
        mod_spec = pl.BlockSpec((tm, d), lambda i, e: (i, 0))
    uspec = pl.BlockSpec((PEER_HEADS, PEER_N_KEYS, tm), lambda i, e: (0, 0, i))
    u1spec = pl.BlockSpec((PEER_HEADS, te // PEER_N_KEYS, tm), lambda i, e: (0, e, i))
    return pl.pallas_call(
        functools.partial(_peer_kernel, tm=tm, te=te),
        out_shape=jax.ShapeDtypeStruct((n, d), F32),
        grid=(n // tm, n_exp // te),
        in_specs=[u1spec, uspec, pl.BlockSpec((PEER_HEADS, tm), lambda i, e: (0, i)),
                  pl.BlockSpec((tm, d), lambda i, e: (i, 0)),
                  pl.BlockSpec((te, d), lambda i, e: (e, 0)),
                  pl.BlockSpec((d, te), lambda i, e: (0, e)),
                  pl.BlockSpec((tm, d), lambda i, e: (i, 0)), mod_spec,
                  pl.BlockSpec((1, d), lambda i, e: (0, 0))],
        out_specs=pl.BlockSpec((tm, d), lambda i, e: (i, 0)),
        scratch_shapes=[pltpu.VMEM((d, tm), F32), pltpu.VMEM((te, tm), F32),
                        pltpu.VMEM((te, tm), BF16)],
        compiler_params=_cparams(("arbitrary", "arbitrary")),
        name="peer_experts",
    )(u1, u2, th, h2, u_bf, vt_bf, x1, gate2, g_final)


def _fox_prompt(fq, fk, fv, logf_t, t_blk):
    t = fq.shape[0]
    nh, dh = FOX_HEADS, FOX_HEAD_DIM
    f2, hi, mid, lo = forget_cumsum(logf_t, t_blk)

    def heads(a):
        return a.astype(BF16).reshape(t, nh, dh).transpose(1, 0, 2)

    def heads_t(a):
        return a.astype(BF16).reshape(t, nh, dh).transpose(1, 2, 0)

    qa = heads_t(fq * (dh ** -0.5 * LOG2E))
    ka = jnp.concatenate([heads(fk), jnp.stack([hi, mid, lo], axis=-1).astype(BF16),
                          jnp.full((nh, t, 3), -1.0, BF16),
                          jnp.zeros((nh, t, 128 - dh - 6), BF16)], axis=-1)
    vt = heads_t(fv)
    o_t = fox_prompt_attention(qa, ka, vt, f2[:, ::t_blk].reshape(-1), t_blk)
    return o_t.reshape(nh * dh, t).T


def _prep_weights(w_in, b_fox_f, b_mlstm_i, b_mlstm_f):
    fw, mw = FOX_WIDTH, MLSTM_WIDTH
    o = 3 * fw + FOX_HEADS
    g0 = o + 3 * mw
    wm = jnp.concatenate([w_in[:, :3 * fw], w_in[:, o:o + 3 * mw], w_in[:, g0 + 2 * MLSTM_HEADS:]],
                         axis=1).astype(BF16)
    wg = jnp.concatenate([w_in[:, 3 * fw:o], w_in[:, g0:g0 + 2 * MLSTM_HEADS]], axis=1)
    wg = jnp.pad(wg, ((0, 0), (0, GATE_LANES - wg.shape[1])))
    bg = jnp.concatenate([b_fox_f, b_mlstm_i, b_mlstm_f])
    bg = jnp.pad(bg, (0, GATE_LANES - bg.shape[0])).reshape(1, GATE_LANES)
    return wm, wg, bg


def kernel(x_prompt, x_sample, c_prompt, c_sample, cache_fox_k, cache_fox_v, cache_fox_logf, state_mlstm_C, state_mlstm_n, state_mlstm_m, page_table, w_ada, b_ada, g_norm_mix, g_norm_ffn, w_in, b_fox_f, b_mlstm_i, b_mlstm_f, g_fox_out, g_mlstm_out, w_out, w_peer_q, peer_keys, peer_u, peer_v, g_final):
    D = D_MODEL
    T = x_prompt.shape[1]
    B, S = x_sample.shape[0], x_sample.shape[1]
    n_pages = page_table.shape[1]
    H, DH = FOX_HEADS, FOX_HEAD_DIM
    MH, DK = MLSTM_HEADS, MLSTM_HEAD_DIM
    SP = 8
    SEG_PER_BLK = 128 // SP

    wm, wg, bg = _prep_weights(w_in[0], b_fox_f[0], b_mlstm_i[0], b_mlstm_f[0])
    w_out_b = w_out[0].astype(BF16)
    w_pq_b = w_peer_q[0].astype(BF16)
    keys_b = peer_keys[0].reshape(2 * PEER_HEADS, PEER_N_KEYS, PEER_HALF).astype(BF16)
    u_b = peer_u[0].astype(BF16)
    vt_b = peer_v[0].T.astype(BF16)
    lane = jnp.arange(FOX_WIDTH)
    bd64 = jnp.where(lane[:, None] // DH == lane[None, :] // DH, 1.0 / DH, 0.0).astype(BF16)
    bd128 = jnp.where(lane[:, None] // DK == lane[None, :] // DK, 1.0 / DK, 0.0).astype(BF16)
    g_mix = g_norm_mix[0].reshape(1, D)
    g_ffn = g_norm_ffn[0].reshape(1, D)
    g_fox = g_fox_out[0].reshape(1, FOX_WIDTH)
    g_ml = g_mlstm_out[0].reshape(1, MLSTM_WIDTH)
    g_fin = g_final.reshape(1, D)

    c_all = jnp.concatenate([c_prompt, c_sample], axis=0)
    c_all = jnp.pad(c_all, ((0, (-c_all.shape[0]) % 8), (0, 0)))
    ada = adaln_terms(c_all, w_ada[0], b_ada[0])

    def terms(a):
        return [a[:, i * D:(i + 1) * D] for i in range(6)]

    shift1_p, scale1_p, gate1_p, shift2_p, scale2_p, gate2_p = terms(ada[0:1])
    shift1_s, scale1_s, gate1_s, shift2_s, scale2_s, gate2_s = terms(
        jnp.repeat(ada[1:1 + B], S, axis=0))

    xp = x_prompt.reshape(T, D)
    fq, fk, fv, mq, mk, mv, mo, gt, gtt = in_projection(xp, scale1_p, shift1_p, g_mix, wm, wg, bg, 512)
    fox_o = _fox_prompt(fq, fk, fv, gtt[0:H], 512)
    m_h, c_p, n_p, m_p = mlstm(
        mq.reshape(1, T, MLSTM_WIDTH), mk.reshape(1, T, MLSTM_WIDTH), mv.reshape(1, T, MLSTM_WIDTH),
        gt.reshape(1, T, GATE_LANES), gtt[0:16].reshape(1, 16, T),
        jnp.zeros((1, MH, DK, DK), F32), jnp.zeros((1, MH, 1, DK), F32), jnp.zeros((1, MH, 1, 1), F32),
        256, 256)
    x1, h2, st = combine(fox_o, m_h.reshape(T, MLSTM_WIDTH), mo, xp, gate1_p, scale2_p, shift2_p,
                         g_fox, g_ml, g_ffn, w_out_b, w_pq_b, keys_b, bd64, bd128, 512)
    u1, u2, th = peer_topk(st)
    y_p = peer_experts(u1, u2, th, h2, u_b, vt_b, x1, gate2_p, g_fin, 512, 2048)

    ns = B * S
    xs = x_sample.reshape(ns, D)
    sfq, sfk, sfv, smq, smk, smv, smo, sgt, _ = in_projection(
        xs, scale1_s, shift1_s, g_mix, wm, wg, bg, 256)
    def pad_tok(a):
        a = a.reshape(B, S, a.shape[-1])
        return jnp.pad(a, ((0, 0), (0, SP - S), (0, 0)))

    q4 = (sfq * DH ** -0.5).reshape(B, S, H, 1, DH)
    qbd = (q4 * jnp.eye(H, dtype=F32)[None, None, :, :, None]).astype(BF16).reshape(B, S * H, FOX_WIDTH)
    lf_new_t = jnp.pad(sgt[:, 0:H].reshape(B, S, H).transpose(0, 2, 1),
                       ((0, 0), (0, 0), (0, PAGE_SIZE - S)))
    n_pool = cache_fox_k.shape[1]
    lf_pages = rows_cumsum(cache_fox_logf.transpose(0, 1, 3, 2).reshape(n_pool * H, PAGE_SIZE), 8192)
    so = fox_sample_attention(
        page_table, qbd, pad_tok(sfk), pad_tok(sfv), lf_new_t,
        cache_fox_k.transpose(0, 1, 3, 4, 2), cache_fox_v.transpose(0, 1, 3, 4, 2),
        lf_pages.reshape(1, n_pool, H, PAGE_SIZE), 16)
    pad_gate = jnp.zeros((GATE_LANES,), F32).at[H:H + MH].set(NEG_BIG)
    sg = jnp.concatenate([sgt.reshape(B, S, GATE_LANES),
                          jnp.broadcast_to(pad_gate, (B, SP - S, GATE_LANES))], axis=1)
    sg = sg.reshape(B // SEG_PER_BLK, 128, GATE_LANES)

    def blk(a):
        return pad_tok(a).reshape(B // SEG_PER_BLK, 128, a.shape[-1])

    sm_h, c_s, n_s, m_s = mlstm(
        blk(smq), blk(smk), blk(smv), sg, sg.transpose(0, 2, 1)[:, 0:16],
        state_mlstm_C[0], state_mlstm_n[0].reshape(B, MH, 1, DK), state_mlstm_m[0].reshape(B, MH, 1, 1),
        128, SP)
    sm_h = sm_h.reshape(B, SP, MLSTM_WIDTH)[:, 0:S].reshape(ns, MLSTM_WIDTH)
    sx1, sh2, sst = combine(so.reshape(ns, FOX_WIDTH), sm_h, smo, xs, gate1_s, scale2_s, shift2_s,
                            g_fox, g_ml, g_ffn, w_out_b, w_pq_b, keys_b, bd64, bd128, 256)
    su1, su2, sth = peer_topk(sst)
    y_s = peer_experts(su1, su2, sth, sh2, u_b, vt_b, sx1, gate2_s, g_fin, 512, 1024)

    return (y_p.reshape(1, T, D), y_s.reshape(B, S, D),
            fk.reshape(1, 1, T, H, DH), fv.reshape(1, 1, T, H, DH), gt[:, 0:H].reshape(1, 1, T, H),
            c_p.reshape(1, 1, MH, DK, DK), n_p.reshape(1, 1, MH, DK), m_p.reshape(1, 1, MH),
            sfk.reshape(1, B, S, H, DH), sfv.reshape(1, B, S, H, DH), sgt[:, 0:H].reshape(1, B, S, H),
            c_s.reshape(1, B, MH, DK, DK), n_s.reshape(1, B, MH, DK), m_s.reshape(1, B, MH))
```
